```python
import math
import jax, jax.numpy as jnp
from jax import lax
import numpy as np

D_MODEL = 2048
BATCH = 2
SEQ = 16384
DEPTH = 1
DEC_BATCH = 1
DEC_SEQ = 16384
PAST_LEN = 128

DILATED_GROUPS = ((128, 1), (512, 4), (2048, 16))
N_GROUPS = len(DILATED_GROUPS)
HEAD_DIM = 64
ATTN_HEADS_PER_GROUP = D_MODEL // 256
ATTN_GROUP_WIDTH = ATTN_HEADS_PER_GROUP * HEAD_DIM
ATTN_WIDTH = N_GROUPS * ATTN_GROUP_WIDTH
BLK = 64
ROPE_THETA = 10000.0
CONV_CHANNELS = D_MODEL // 2
CONV_WIDTH = 31
N_BRANCHES = 2
IN_COLS = 3 * ATTN_WIDTH + 2 * CONV_CHANNELS + N_BRANCHES * D_MODEL
PEER_HEADS = 8
PEER_KEY_DIM = 256
PEER_KEY_HALF = PEER_KEY_DIM // 2
N_KEYS = 128
N_EXPERTS = N_KEYS * N_KEYS
PEER_TOPK = 16
PEER_CHUNK = 128
LN_EPS = 1e-5
DEEPNORM_ALPHA = (2.0 * DEPTH) ** 0.25
DEEPNORM_BETA = (8.0 * DEPTH) ** -0.25

kernel_name = "hybrid_dilated_attn_conformer_peer_encoder"


def layer_norm(x, g, b):
    xf = x.astype(jnp.float32)
    mu = jnp.mean(xf, axis=-1, keepdims=True)
    xc = xf - mu
    var = jnp.mean(xc * xc, axis=-1, keepdims=True)
    return (xc * lax.rsqrt(var + LN_EPS) * g + b).astype(x.dtype)


def rope(x):
    S = x.shape[1]
    half = x.shape[-1] // 2
    inv_freq = ROPE_THETA ** (-jnp.arange(half, dtype=jnp.float32) / half)
    ang = jnp.arange(S, dtype=jnp.float32)[:, None] * inv_freq[None, :]
    cos = jnp.cos(ang)[None, :, None, None, :]
    sin = jnp.sin(ang)[None, :, None, None, :]
    xf = x.astype(jnp.float32)
    x1, x2 = xf[..., :half], xf[..., half:]
    return jnp.concatenate([x1 * cos - x2 * sin, x2 * cos + x1 * sin], axis=-1).astype(x.dtype)


def _neighbour_blocks(t):
    tp = jnp.pad(t, ((0, 0), (0, 0), (1, 1), (0, 0), (0, 0), (0, 0)))
    return jnp.concatenate([tp[:, :, :-2], tp[:, :, 1:-1], tp[:, :, 2:]], axis=3)


def dilated_window_attention(q, k, v, window, dilation):
    B, S, H, C = q.shape
    half = window // (2 * dilation)
    span = dilation * BLK
    l_pad = -(-S // span) * span
    sub_len = l_pad // dilation
    nb = sub_len // BLK

    def to_blocks(t):
        t = jnp.pad(t, ((0, 0), (0, l_pad - S), (0, 0), (0, 0)))
        t = t.reshape(B, sub_len, dilation, H, C).transpose(0, 2, 1, 3, 4)
        return t.reshape(B, dilation, nb, BLK, H, C)

    qb = to_blocks(q)
    kn = _neighbour_blocks(to_blocks(k))
    vn = _neighbour_blocks(to_blocks(v))
    s = jnp.einsum('brnqhc,brnkhc->brnhqk', qb, kn).astype(jnp.float32) * (1.0 / math.sqrt(C))

    blk_idx = jnp.arange(nb, dtype=jnp.int32)[:, None]
    qi = blk_idx * BLK + jnp.arange(BLK, dtype=jnp.int32)[None, :]
    kj = (blk_idx - 1) * BLK + jnp.arange(3 * BLK, dtype=jnp.int32)[None, :]
    resid = jnp.arange(dilation, dtype=jnp.int32)[:, None, None]
    key_pos = kj[None] * dilation + resid
    valid = ((kj >= 0)[None, :, None, :]
             & (key_pos < S)[:, :, None, :]
             & (jnp.abs(qi[:, :, None] - kj[:, None, :]) <= half)[None])
    valid = valid[None, :, :, None, :, :]

    s = jnp.where(valid, s, -jnp.inf)
    m = jnp.max(s, axis=-1, keepdims=True)
    p = jnp.exp(s - m)
    l = jnp.sum(p, axis=-1, keepdims=True)
    o = jnp.einsum('brnhqk,brnkhc->brnqhc', (p / l).astype(v.dtype), vn)
    lse = (m + jnp.log(l))[..., 0]

    o = o.reshape(B, dilation, sub_len, H, C).transpose(0, 2, 1, 3, 4).reshape(B, l_pad, H, C)[:, :S]
    lse = lse.transpose(0, 1, 2, 4, 3).reshape(B, dilation, sub_len, H)
    lse = lse.transpose(0, 2, 1, 3).reshape(B, l_pad, H)[:, :S]
    return o, lse


def depthwise_conv(x, w, b):
    pad = CONV_WIDTH // 2
    y = lax.conv_general_dilated(x, w[:, None, :], window_strides=(1,), padding=[(pad, pad)],
                                 dimension_numbers=('NWC', 'WIO', 'NWC'),
                                 feature_group_count=x.shape[-1])
    return y + b


def token_mixers(x, w_in, w_attn_o, conv_w, conv_b, conv_ln_g, conv_ln_b, w_conv_o, w_out):
    B, S, _ = x.shape
    proj = jnp.einsum('bsd,de->bse', x, w_in)
    o0 = 0
    q = proj[..., o0:o0 + ATTN_WIDTH].reshape(B, S, N_GROUPS, ATTN_HEADS_PER_GROUP, HEAD_DIM)
    o0 += ATTN_WIDTH
    k = proj[..., o0:o0 + ATTN_WIDTH].reshape(B, S, N_GROUPS, ATTN_HEADS_PER_GROUP, HEAD_DIM)
    o0 += ATTN_WIDTH
    v = proj[..., o0:o0 + ATTN_WIDTH].reshape(B, S, N_GROUPS, ATTN_HEADS_PER_GROUP, HEAD_DIM)
    o0 += ATTN_WIDTH
    glu_a = proj[..., o0:o0 + CONV_CHANNELS]
    o0 += CONV_CHANNELS
    glu_b = proj[..., o0:o0 + CONV_CHANNELS]
    o0 += CONV_CHANNELS
    gates = jax.nn.sigmoid(proj[..., o0:].reshape(B, S, N_BRANCHES, D_MODEL))

    q = rope(q)
    k = rope(k)
    outs, lses = [], []
    for g, (window, dilation) in enumerate(DILATED_GROUPS):
        o_g, lse_g = dilated_window_attention(q[:, :, g], k[:, :, g], v[:, :, g], window, dilation)
        outs.append(o_g)
        lses.append(lse_g)
    wts = jax.nn.softmax(jnp.stack(lses, axis=0), axis=0)
    attn = jnp.sum(wts[..., None].astype(v.dtype) * jnp.stack(outs, axis=0), axis=0)
    branch_a = jnp.einsum('bse,ed->bsd', attn.reshape(B, S, ATTN_GROUP_WIDTH), w_attn_o)

    hc = glu_a * jax.nn.sigmoid(glu_b)
    hc = depthwise_conv(hc, conv_w, conv_b)
    hc = jax.nn.silu(layer_norm(hc, conv_ln_g, conv_ln_b))
    branch_b = jnp.einsum('bsc,cd->bsd', hc, w_conv_o)

    merged = gates[:, :, 0] * branch_a + gates[:, :, 1] * branch_b
    return jnp.einsum('bsd,de->bse', merged, w_out)


def peer(x, w_q, sub_keys, u_tab, v_tab):
    B, S, D = x.shape
    T = B * S
    xf = x.reshape(T, D)
    q = jnp.einsum('td,de->te', xf, w_q).reshape(T, PEER_HEADS, 2, PEER_KEY_HALF)
    s = jnp.einsum('thpc,pnc->thpn', q, sub_keys).astype(jnp.float32)
    s_top, i_top = lax.top_k(s, PEER_TOPK)
    cand_s = (s_top[:, :, 0, :, None] + s_top[:, :, 1, None, :]).reshape(T, PEER_HEADS, PEER_TOPK * PEER_TOPK)
    cand_i = (i_top[:, :, 0, :, None] * N_KEYS + i_top[:, :, 1, None, :]).reshape(T, PEER_HEADS, PEER_TOPK * PEER_TOPK)
    best_s, best_pos = lax.top_k(cand_s, PEER_TOPK)
    experts = jnp.take_along_axis(cand_i, best_pos, axis=-1)
    gate = jax.nn.softmax(best_s, axis=-1).astype(x.dtype)

    n_chunks = T // PEER_CHUNK
    hk = PEER_HEADS * PEER_TOPK
    xc = xf.reshape(n_chunks, PEER_CHUNK, D)
    ec = experts.reshape(n_chunks, PEER_CHUNK, hk)
    gc = gate.reshape(n_chunks, PEER_CHUNK, hk)

    def expert_block(args):
        xb, eb, gb = args
        u = u_tab[eb]
        h = jax.nn.gelu(jnp.einsum('tkd,td->tk', u, xb), approximate=False)
        vv = v_tab[eb]
        return jnp.einsum('tk,tkd->td', gb * h, vv)

    out = lax.map(expert_block, (xc, ec, gc))
    return out.reshape(B, S, D)


def encoder_layer(x, w_in, w_attn_o, conv_w, conv_b, conv_ln_g, conv_ln_b, w_conv_o, w_out,
                  ln1_g, ln1_b, peer_w_q, peer_sub_keys, peer_u, peer_v, ln2_g, ln2_b):
    mix = token_mixers(x, w_in, w_attn_o, conv_w, conv_b, conv_ln_g, conv_ln_b, w_conv_o, w_out)
    x = layer_norm(DEEPNORM_ALPHA * x + mix, ln1_g, ln1_b)
    ff = peer(x, peer_w_q, peer_sub_keys, peer_u, peer_v)
    return layer_norm(DEEPNORM_ALPHA * x + ff, ln2_g, ln2_b)


def setup_inputs(seed: int = 0) -> dict:
    key = jax.random.key(seed)
    ks = jax.random.split(key, 20)

    def nrm(k, shape, scale):
        return jax.random.normal(k, shape, jnp.float32) * scale

    L = DEPTH
    return {
        "x_prompt": nrm(ks[0], (BATCH, SEQ, D_MODEL), 1.0),
        "x_sample": nrm(ks[1], (DEC_BATCH, DEC_SEQ, D_MODEL), 1.0),
        "w_in": nrm(ks[2], (L, D_MODEL, IN_COLS), D_MODEL ** -0.5),
        "w_attn_o": nrm(ks[3], (L, ATTN_GROUP_WIDTH, D_MODEL), ATTN_GROUP_WIDTH ** -0.5),
        "conv_w": nrm(ks[4], (L, CONV_WIDTH, CONV_CHANNELS), CONV_WIDTH ** -0.5),
        "conv_b": nrm(ks[5], (L, CONV_CHANNELS), 0.01),
        "conv_ln_g": 1.0 + nrm(ks[6], (L, CONV_CHANNELS), 0.01),
        "conv_ln_b": nrm(ks[7], (L, CONV_CHANNELS), 0.01),
        "w_conv_o": nrm(ks[8], (L, CONV_CHANNELS, D_MODEL), CONV_CHANNELS ** -0.5),
        "w_out": nrm(ks[9], (L, D_MODEL, D_MODEL), DEEPNORM_BETA * D_MODEL ** -0.5),
        "ln1_g": 1.0 + nrm(ks[10], (L, D_MODEL), 0.01),
        "ln1_b": nrm(ks[11], (L, D_MODEL), 0.01),
        "peer_w_q": nrm(ks[12], (L, D_MODEL, PEER_HEADS * PEER_KEY_DIM), D_MODEL ** -0.5),
        "peer_sub_keys": nrm(ks[13], (L, 2, N_KEYS, PEER_KEY_HALF), PEER_KEY_HALF ** -0.5),
        "peer_u": nrm(ks[14], (L, N_EXPERTS, D_MODEL), D_MODEL ** -0.5),
        "peer_v": nrm(ks[15], (L, N_EXPERTS, D_MODEL), DEEPNORM_BETA * PEER_HEADS ** -0.5),
        "ln2_g": 1.0 + nrm(ks[16], (L, D_MODEL), 0.01),
        "ln2_b": nrm(ks[17], (L, D_MODEL), 0.01),
    }


def reference(x_prompt, x_sample, w_in, w_attn_o, conv_w, conv_b, conv_ln_g, conv_ln_b, w_conv_o, w_out,
              ln1_g, ln1_b, peer_w_q, peer_sub_keys, peer_u, peer_v, ln2_g, ln2_b):
    def trunk(x):
        for l in range(DEPTH):
            x = encoder_layer(x, w_in[l], w_attn_o[l], conv_w[l], conv_b[l], conv_ln_g[l], conv_ln_b[l],
                              w_conv_o[l], w_out[l], ln1_g[l], ln1_b[l], peer_w_q[l], peer_sub_keys[l],
                              peer_u[l], peer_v[l], ln2_g[l], ln2_b[l])
        return x

    y_prompt = trunk(x_prompt)
    y_sample = trunk(x_sample)
    return (y_prompt, y_sample)
```

```python
import functools
import math

import jax
import jax.numpy as jnp
from jax import lax
from jax.experimental import pallas as pl
from jax.experimental.pallas import tpu as pltpu

D = 2048
HEAD_DIM = 64
HEADS = 8
GROUP_W = HEADS * HEAD_DIM
DILATED_GROUPS = ((128, 1), (512, 4), (2048, 16))
N_GROUPS = len(DILATED_GROUPS)
ATTN_W = N_GROUPS * GROUP_W
BLK = 64
ROPE_THETA = 10000.0
CONV_C = D // 2
CONV_K = 31
IN_COLS = 3 * ATTN_W + 2 * CONV_C + 2 * D
GLU_COL0 = 3 * ATTN_W
GATE_COL0 = GLU_COL0 + 2 * CONV_C
PEER_HEADS = 8
N_KEYS = 128
N_EXPERTS = N_KEYS * N_KEYS
TOPK = 16
LN_EPS = 1e-5
DEPTH = 1
ALPHA = (2.0 * DEPTH) ** 0.25

LANES = 128
SUBLANES = 8
VMEM_LIMIT = 56 * 1024 * 1024

F32 = jnp.float32
BF16 = jnp.bfloat16
NEG_INF = float("-inf")


def _params(semantics):
    return pltpu.CompilerParams(dimension_semantics=semantics, vmem_limit_bytes=VMEM_LIMIT)


def _layer_norm(y, g, b):
    mu = jnp.mean(y, axis=-1, keepdims=True)
    yc = y - mu
    var = jnp.mean(yc * yc, axis=-1, keepdims=True)
    return yc * lax.rsqrt(var + LN_EPS) * g + b


def _qkv_body(x_ref, w_ref, cos_ref, sin_ref, o_ref, xb_ref):
    j = pl.program_id(1)

    @pl.when(j == 0)
    def _():
        xb_ref[...] = x_ref[...].astype(BF16)

    acc = jnp.dot(xb_ref[...], w_ref[...], preferred_element_type=F32)

    @pl.when(j == 2)
    def _():
        o_ref[...] = acc.astype(o_ref.dtype)

    @pl.when(j < 2)
    def _():
        scale = jnp.where(j == 0, 1.0 / math.sqrt(HEAD_DIM), 1.0).astype(F32)
        c = cos_ref[...] * scale
        s = sin_ref[...] * scale
        lane = lax.broadcasted_iota(jnp.int32, c.shape, 1)
        first_half = (lane % HEAD_DIM) < (HEAD_DIM // 2)
        for cb in range(ATTN_W // LANES):
            xc = acc[:, cb * LANES:(cb + 1) * LANES]
            partner = jnp.where(first_half,
                                pltpu.roll(xc, LANES - HEAD_DIM // 2, 1),
                                pltpu.roll(xc, HEAD_DIM // 2, 1))
            o_ref[:, cb * LANES:(cb + 1) * LANES] = (xc * c + partner * s).astype(o_ref.dtype)


def _qkv_proj(x2, w_in_bf, cos_t, sin_t, seq):
    t = x2.shape[0]
    tm = 512
    nseq = seq // tm
    return pl.pallas_call(
        _qkv_body,
        grid=(t // tm, 3),
        in_specs=[
            pl.BlockSpec((tm, D), lambda i, j: (i, 0)),
            pl.BlockSpec((D, ATTN_W), lambda i, j: (0, j)),
            pl.BlockSpec((tm, LANES), lambda i, j: (i % nseq, 0)),
            pl.BlockSpec((tm, LANES), lambda i, j: (i % nseq, 0)),
        ],
        out_specs=pl.BlockSpec((tm, ATTN_W), lambda i, j: (i, j)),
        out_shape=jax.ShapeDtypeStruct((t, 3 * ATTN_W), BF16),
        scratch_shapes=[pltpu.VMEM((tm, D), BF16)],
        compiler_params=_params(("parallel", "arbitrary")),
        name="qkv_proj",
    )(x2, w_in_bf, cos_t, sin_t)


def _glu_body(x_ref, wa_ref, wb_ref, o_ref, xb_ref):
    @pl.when(pl.program_id(1) == 0)
    def _():
        xb_ref[...] = x_ref[...].astype(BF16)

    xb = xb_ref[...]
    a = jnp.dot(xb, wa_ref[...], preferred_element_type=F32)
    b = jnp.dot(xb, wb_ref[...], preferred_element_type=F32)
    o_ref[...] = a * jax.nn.sigmoid(b)


def _glu_proj(x2, w_in_bf):
    t = x2.shape[0]
    tm, tn = 512, 512
    a0 = GLU_COL0 // tn
    b0 = (GLU_COL0 + CONV_C) // tn
    return pl.pallas_call(
        _glu_body,
        grid=(t // tm, CONV_C // tn),
        in_specs=[
            pl.BlockSpec((tm, D), lambda i, j: (i, 0)),
            pl.BlockSpec((D, tn), lambda i, j: (0, a0 + j)),
            pl.BlockSpec((D, tn), lambda i, j: (0, b0 + j)),
        ],
        out_specs=pl.BlockSpec((tm, tn), lambda i, j: (i, j)),
        out_shape=jax.ShapeDtypeStruct((t, CONV_C), F32),
        scratch_shapes=[pltpu.VMEM((tm, D), BF16)],
        compiler_params=_params(("parallel", "arbitrary")),
        name="glu_proj",
    )(x2, w_in_bf, w_in_bf)


def _attn_body(q_ref, kp_ref, km_ref, kn_ref, vp_ref, vm_ref, vn_ref, o_ref, lse_ref, *, sub_len):
    n = pl.program_id(2)
    tq = q_ref.shape[0]
    tk = tq + 2 * BLK
    q = q_ref[...]
    k = jnp.concatenate([kp_ref[...], km_ref[...], kn_ref[...]], axis=0)
    v = jnp.concatenate([vp_ref[...], vm_ref[...], vn_ref[...]], axis=0)
    row = lax.broadcasted_iota(jnp.int32, (tq, tk), 0)
    col = lax.broadcasted_iota(jnp.int32, (tq, tk), 1)
    pos = n * tq - BLK + col
    valid = (col >= row) & (col <= row + 2 * BLK) & (pos >= 0) & (pos < sub_len)
    for h in range(HEADS):
        sl = slice(h * HEAD_DIM, (h + 1) * HEAD_DIM)
        s = lax.dot_general(q[:, sl], k[:, sl], (((1,), (1,)), ((), ())), preferred_element_type=F32)
        s = jnp.where(valid, s, NEG_INF)
        m = jnp.max(s, axis=-1, keepdims=True)
        p = jnp.exp(s - m)
        l = jnp.sum(p, axis=-1, keepdims=True)
        o = jnp.dot(p.astype(BF16), v[:, sl], preferred_element_type=F32)
        o_ref[:, sl] = o / l
        lse_ref[:, sl] = jnp.broadcast_to(m + jnp.log(l), (tq, HEAD_DIM))


def _dilated_attention(qkv, g, batch, seq):
    window, d = DILATED_GROUPS[g]
    assert window // (2 * d) == BLK and seq % (d * BLK) == 0
    sub_len = seq // d
    tq = min(256, sub_len)
    nb = sub_len // BLK
    cols = 3 * N_GROUPS
    qkv_v = qkv.reshape(batch, sub_len, d * 3 * ATTN_W)

    def main(off):
        return pl.BlockSpec((None, tq, GROUP_W), lambda b, r, n: (b, n, r * cols + off + g))

    def prev(off):
        return pl.BlockSpec((None, BLK, GROUP_W),
                            lambda b, r, n: (b, jnp.maximum(n * (tq // BLK) - 1, 0), r * cols + off + g))

    def nxt(off):
        return pl.BlockSpec((None, BLK, GROUP_W),
                            lambda b, r, n: (b, jnp.minimum((n + 1) * (tq // BLK), nb - 1), r * cols + off + g))

    out_spec = pl.BlockSpec((None, tq, GROUP_W), lambda b, r, n: (b, n, r))
    o, lse = pl.pallas_call(
        functools.partial(_attn_body, sub_len=sub_len),
        grid=(batch, d, sub_len // tq),
        in_specs=[main(0), prev(N_GROUPS), main(N_GROUPS), nxt(N_GROUPS),
                  prev(2 * N_GROUPS), main(2 * N_GROUPS), nxt(2 * N_GROUPS)],
        out_specs=[out_spec, out_spec],
        out_shape=[jax.ShapeDtypeStruct((batch, sub_len, d * GROUP_W), F32)] * 2,
        compiler_params=_params(("parallel", "parallel", "arbitrary")),
        name=f"dilated_attn_{g}",
    )(qkv_v, qkv_v, qkv_v, qkv_v, qkv_v, qkv_v, qkv_v)
    return o.reshape(batch * seq, GROUP_W), lse.reshape(batch * seq, GROUP_W)


CONV_HALO = 16


def _conv_body(hp_ref, hm_ref, hn_ref, w_ref, b_ref, g_ref, beta_ref, o_ref, win_ref, y_ref):
    n = pl.program_id(1)
    tc = hm_ref.shape[0]
    win_ref[0:CONV_HALO, :] = jnp.where(n > 0, hp_ref[...], 0.0)
    win_ref[CONV_HALO:CONV_HALO + tc, :] = hm_ref[...]
    win_ref[CONV_HALO + tc:, :] = jnp.where(n < pl.num_programs(1) - 1, hn_ref[...], 0.0)
    off = CONV_HALO - CONV_K // 2
    for cb in range(CONV_C // LANES):
        cs = slice(cb * LANES, (cb + 1) * LANES)
        acc = jnp.zeros((tc, LANES), F32)
        for j in range(CONV_K):
            acc = acc + win_ref[off + j:off + j + tc, cs] * w_ref[j:j + 1, cs]
        y_ref[:, cs] = acc
    y = _layer_norm(y_ref[...] + b_ref[...], g_ref[...], beta_ref[...])
    o_ref[...] = (y * jax.nn.sigmoid(y)).astype(o_ref.dtype)


def _conv_module(hc, conv_w, conv_b, ln_g, ln_b, batch, seq):
    tc = 256
    nh = seq // CONV_HALO
    hc3 = hc.reshape(batch, seq, CONV_C)
    vec = pl.BlockSpec((1, CONV_C), lambda b, n: (0, 0))
    out = pl.pallas_call(
        _conv_body,
        grid=(batch, seq // tc),
        in_specs=[
            pl.BlockSpec((None, CONV_HALO, CONV_C),
                         lambda b, n: (b, jnp.maximum(n * (tc // CONV_HALO) - 1, 0), 0)),
            pl.BlockSpec((None, tc, CONV_C), lambda b, n: (b, n, 0)),
            pl.BlockSpec((None, CONV_HALO, CONV_C),
                         lambda b, n: (b, jnp.minimum((n + 1) * (tc // CONV_HALO), nh - 1), 0)),
            pl.BlockSpec((CONV_K, CONV_C), lambda b, n: (0, 0)),
            vec, vec, vec,
        ],
        out_specs=pl.BlockSpec((None, tc, CONV_C), lambda b, n: (b, n, 0)),
        out_shape=jax.ShapeDtypeStruct((batch, seq, CONV_C), BF16),
        scratch_shapes=[pltpu.VMEM((tc + 2 * CONV_HALO, CONV_C), F32), pltpu.VMEM((tc, CONV_C), F32)],
        compiler_params=_params(("parallel", "arbitrary")),
        name="conv_module",
    )(hc3, hc3, hc3, conv_w, conv_b.reshape(1, -1), ln_g.reshape(1, -1), ln_b.reshape(1, -1))
    return out.reshape(batch * seq, CONV_C)


def _attn_merge_body(o0_ref, o1_ref, o2_ref, l0_ref, l1_ref, l2_ref, attn_ref):
    l0, l1, l2 = l0_ref[...], l1_ref[...], l2_ref[...]
    m = jnp.maximum(jnp.maximum(l0, l1), l2)
    e0, e1, e2 = jnp.exp(l0 - m), jnp.exp(l1 - m), jnp.exp(l2 - m)
    attn = (e0 * o0_ref[...] + e1 * o1_ref[...] + e2 * o2_ref[...]) / (e0 + e1 + e2)
    attn_ref[...] = attn.astype(attn_ref.dtype)


def _attn_merge(attn_parts):
    (o0, l0), (o1, l1), (o2, l2) = attn_parts
    t = o0.shape[0]
    tm = 1024
    blk = pl.BlockSpec((tm, GROUP_W), lambda i: (i, 0))
    return pl.pallas_call(
        _attn_merge_body,
        grid=(t // tm,),
        in_specs=[blk] * 6,
        out_specs=blk,
        out_shape=jax.ShapeDtypeStruct((t, GROUP_W), BF16),
        compiler_params=_params(("parallel",)),
        name="attn_merge",
    )(o0, o1, o2, l0, l1, l2)


def _merge_body(x_ref, attn_ref, hcn_ref, wga_ref, wgb_ref, wao_ref, wco_ref, wout_ref, g_ref, b_ref,
                x1_ref, x1t_ref, xb_ref, acc_ref):
    j = pl.program_id(1)

    @pl.when(j == 0)
    def _():
        xb_ref[...] = x_ref[...].astype(BF16)
        acc_ref[...] = jnp.zeros_like(acc_ref)

    xb = xb_ref[...]
    ga = jax.nn.sigmoid(jnp.dot(xb, wga_ref[...], preferred_element_type=F32))
    gb = jax.nn.sigmoid(jnp.dot(xb, wgb_ref[...], preferred_element_type=F32))
    br_a = jnp.dot(attn_ref[...], wao_ref[...], preferred_element_type=F32)
    br_b = jnp.dot(hcn_ref[...], wco_ref[...], preferred_element_type=F32)
    merged = (ga * br_a + gb * br_b).astype(BF16)
    acc_ref[...] += jnp.dot(merged, wout_ref[...], preferred_element_type=F32)

    @pl.when(j == pl.num_programs(1) - 1)
    def _():
        x1 = _layer_norm(ALPHA * x_ref[...] + acc_ref[...], g_ref[...], b_ref[...])
        x1_ref[...] = x1
        x1t_ref[...] = x1.T.astype(BF16)


def _merge_layer(x2, attn, hcn, w_in_bf, w_ao_bf, w_co_bf, w_out_bf, ln_g, ln_b):
    t = x2.shape[0]
    tm, tn = 512, 512
    nj = D // tn
    ga0 = GATE_COL0 // tn
    gb0 = (GATE_COL0 + D) // tn
    vec = pl.BlockSpec((1, D), lambda i, j: (0, 0))
    return pl.pallas_call(
        _merge_body,
        grid=(t // tm, nj),
        in_specs=[
            pl.BlockSpec((tm, D), lambda i, j: (i, 0)),
            pl.BlockSpec((tm, GROUP_W), lambda i, j: (i, 0)),
            pl.BlockSpec((tm, CONV_C), lambda i, j: (i, 0)),
            pl.BlockSpec((D, tn), lambda i, j: (0, ga0 + j)),
            pl.BlockSpec((D, tn), lambda i, j: (0, gb0 + j)),
            pl.BlockSpec((GROUP_W, tn), lambda i, j: (0, j)),
            pl.BlockSpec((CONV_C, tn), lambda i, j: (0, j)),
            pl.BlockSpec((tn, D), lambda i, j: (j, 0)),
            vec, vec,
        ],
        out_specs=[pl.BlockSpec((tm, D), lambda i, j: (i, 0)),
                   pl.BlockSpec((D, tm), lambda i, j: (0, i))],
        out_shape=[jax.ShapeDtypeStruct((t, D), F32), jax.ShapeDtypeStruct((D, t), BF16)],
        scratch_shapes=[pltpu.VMEM((tm, D), BF16), pltpu.VMEM((tm, D), F32)],
        compiler_params=_params(("parallel", "arbitrary")),
        name="merge_layer",
    )(x2, attn, hcn, w_in_bf, w_in_bf, w_ao_bf, w_co_bf, w_out_bf, ln_g.reshape(1, -1), ln_b.reshape(1, -1))


def _top16(s):
    row = lax.broadcasted_iota(jnp.int32, s.shape, 0).astype(F32)
    rank = jnp.full(s.shape, float(TOPK), F32)
    vals = []
    for k in range(TOPK):
        m = jnp.max(s, axis=0, keepdims=True)
        first = jnp.min(jnp.where(s == m, row, float(N_KEYS)), axis=0, keepdims=True)
        sel = row == first
        rank = jnp.where(sel, float(k), rank)
        s = jnp.where(sel, NEG_INF, s)
        vals.append(m)
    return rank, jnp.concatenate(vals, axis=0)


def _pair_counts(a1, a2):
    half = TOPK // 2
    top = a1[0:1, :] + a2[0:1, :]
    blocks = [a1 + a2[0:1, :]]
    rows_a = [lax.broadcasted_iota(jnp.int32, (TOPK, LANES), 0)]
    rows_b = [jnp.zeros((TOPK, LANES), jnp.int32)]
    for b in range(1, TOPK):
        blocks.append(a1[0:half, :] + a2[b:b + 1, :])
        rows_a.append(lax.broadcasted_iota(jnp.int32, (half, LANES), 0))
        rows_b.append(jnp.full((half, LANES), b, jnp.int32))
    ra = jnp.concatenate(rows_a, axis=0)
    rb = jnp.concatenate(rows_b, axis=0)
    cand0 = jnp.where((ra + 1) * (rb + 1) <= TOPK, jnp.concatenate(blocks, axis=0), NEG_INF)
    pos = (ra * TOPK + rb).astype(F32)
    cand = cand0
    for _ in range(TOPK):
        m = jnp.max(cand, axis=0, keepdims=True)
        first = jnp.min(jnp.where(cand == m, pos, float(TOPK * TOPK)), axis=0, keepdims=True)
        cand = jnp.where(pos == first, NEG_INF, cand)
    chosen = cand != cand0
    cnt = chosen.astype(F32)
    n_lo = cnt[0:half, :]
    for b in range(1, TOPK):
        n_lo = n_lo + cnt[TOPK + (b - 1) * half:TOPK + b * half, :]
    n = jnp.concatenate([n_lo, cnt[half:TOPK, :]], axis=0)
    z = jnp.sum(jnp.where(chosen, jnp.exp(cand0 - top), 0.0), axis=0, keepdims=True)
    return n, z


def _route_body(x1t_ref, wq_ref, keys_ref, r2_ref, e2_ref, n1_ref, e1_ref, q_ref, sc_ref):
    tm = x1t_ref.shape[1]
    ns = tm // LANES
    q_ref[...] = jnp.dot(wq_ref[...], x1t_ref[...], preferred_element_type=F32).astype(BF16)
    for hp in range(2 * PEER_HEADS):
        st = jnp.dot(keys_ref[hp % 2], q_ref[hp * N_KEYS:(hp + 1) * N_KEYS, :],
                     preferred_element_type=F32)
        for sl in range(ns):
            sc_ref[hp // 2, sl, hp % 2] = st[:, sl * LANES:(sl + 1) * LANES]

    def per_head_slab(idx, carry):
        h = idx // ns
        sl = idx % ns
        s1 = sc_ref[h, sl, 0]
        s2 = sc_ref[h, sl, 1]
        r1, a1 = _top16(s1)
        r2, a2 = _top16(s2)
        n, z = _pair_counts(a1, a2)
        n1 = jnp.zeros_like(r1)
        for a in range(TOPK):
            n1 = jnp.where(r1 == float(a), n[a:a + 1, :], n1)
        e1 = jnp.where(r1 < float(TOPK), jnp.exp(s1 - a1[0:1, :]), 0.0)
        e2 = jnp.where(r2 < float(TOPK), jnp.exp(s2 - a2[0:1, :]), 0.0) / z
        r2_ref[h, sl] = r2.astype(r2_ref.dtype)
        e2_ref[h, sl] = e2.astype(e2_ref.dtype)
        n1_ref[h, sl] = n1
        e1_ref[h, sl] = e1
        return carry

    lax.fori_loop(0, PEER_HEADS * ns, per_head_slab, 0)


def _peer_route(x1t, wq_t_bf, keys_bf):
    t = x1t.shape[1]
    tm = 512
    ns = tm // LANES
    blk = pl.BlockSpec((PEER_HEADS, ns, N_KEYS, LANES), lambda i: (0, i, 0, 0))
    shape = (PEER_HEADS, t // LANES, N_KEYS, LANES)
    return pl.pallas_call(
        _route_body,
        grid=(t // tm,),
        in_specs=[
            pl.BlockSpec((D, tm), lambda i: (0, i)),
            pl.BlockSpec((D, D), lambda i: (0, 0)),
            pl.BlockSpec((2, N_KEYS, N_KEYS), lambda i: (0, 0, 0)),
        ],
        out_specs=[blk, blk, blk, blk],
        out_shape=[jax.ShapeDtypeStruct(shape, BF16), jax.ShapeDtypeStruct(shape, BF16),
                   jax.ShapeDtypeStruct(shape, F32), jax.ShapeDtypeStruct(shape, F32)],
        scratch_shapes=[pltpu.VMEM((D, tm), BF16), pltpu.VMEM((PEER_HEADS, ns, 2, N_KEYS, LANES), F32)],
        compiler_params=_params(("parallel",)),
        name="peer_route",
    )(x1t, wq_t_bf, keys_bf)


def _experts_body(x1t_ref, x1_ref, u_ref, vt_ref, r2_ref, e2_ref, n1_ref, e1_ref, g_ref, b_ref,
                  o_ref, acc_ref, gh_ref):
    c = pl.program_id(1)
    tm = x1t_ref.shape[1]
    ns = tm // LANES
    ni = u_ref.shape[0] // N_KEYS

    @pl.when(c == 0)
    def _():
        acc_ref[...] = jnp.zeros_like(acc_ref)

    ht = jnp.dot(u_ref[...], x1t_ref[...], preferred_element_type=F32)
    act = (0.5 * ht * (1.0 + lax.erf(ht * (1.0 / math.sqrt(2.0))))).astype(BF16)
    for ts in range(ns):
        cs = slice(ts * LANES, (ts + 1) * LANES)
        for ii in range(ni):
            i = c * ni + ii
            gate = jnp.zeros((N_KEYS, LANES), BF16)
            for h in range(PEER_HEADS):
                n1 = jnp.broadcast_to(n1_ref[h, ts, pl.ds(i, 1), :], (N_KEYS, LANES)).astype(BF16)
                e1 = jnp.broadcast_to(e1_ref[h, ts, pl.ds(i, 1), :], (N_KEYS, LANES)).astype(BF16)
                gate = gate + jnp.where(r2_ref[h, ts] < n1, e1 * e2_ref[h, ts], jnp.zeros_like(gate))
            gh_ref[ii * N_KEYS:(ii + 1) * N_KEYS, cs] = gate * act[ii * N_KEYS:(ii + 1) * N_KEYS, cs]
    acc_ref[...] += jnp.dot(vt_ref[...], gh_ref[...], preferred_element_type=F32)

    @pl.when(c == pl.num_programs(1) - 1)
    def _():
        y = ALPHA * x1_ref[...] + acc_ref[...].T
        o_ref[...] = _layer_norm(y, g_ref[...], b_ref[...])


def _peer_experts(x1t, x1, u_bf, vt_bf, route, ln_g, ln_b):
    t = x1.shape[0]
    tm, ec = 512, 512
    ns = tm // LANES
    r2, e2, n1, e1 = route
    blk = pl.BlockSpec((PEER_HEADS, ns, N_KEYS, LANES), lambda i, c: (0, i, 0, 0))
    vec = pl.BlockSpec((1, D), lambda i, c: (0, 0))
    return pl.pallas_call(
        _experts_body,
        grid=(t // tm, N_EXPERTS // ec),
        in_specs=[
            pl.BlockSpec((D, tm), lambda i, c: (0, i)),
            pl.BlockSpec((tm, D), lambda i, c: (i, 0)),
            pl.BlockSpec((ec, D), lambda i, c: (c, 0)),
            pl.BlockSpec((D, ec), lambda i, c: (0, c)),
            blk, blk, blk, blk, vec, vec,
        ],
        out_specs=pl.BlockSpec((tm, D), lambda i, c: (i, 0)),
        out_shape=jax.ShapeDtypeStruct((t, D), F32),
        scratch_shapes=[pltpu.VMEM((D, tm), F32), pltpu.VMEM((ec, tm), BF16)],
        compiler_params=_params(("parallel", "arbitrary")),
        name="peer_experts",
    )(x1t, x1, u_bf, vt_bf, r2, e2, n1, e1, ln_g.reshape(1, -1), ln_b.reshape(1, -1))


def _rope_tables(seq):
    half = HEAD_DIM // 2
    inv_freq = ROPE_THETA ** (-jnp.arange(half, dtype=F32) / half)
    ang = jnp.arange(seq, dtype=F32)[:, None] * inv_freq[None, :]
    cos, sin = jnp.cos(ang), jnp.sin(ang)
    reps = LANES // HEAD_DIM
    cos_t = jnp.tile(jnp.concatenate([cos, cos], axis=1), (1, reps))
    sin_t = jnp.tile(jnp.concatenate([-sin, sin], axis=1), (1, reps))
    return cos_t, sin_t


def _encoder_layer(x, w, tables):
    batch, seq, _ = x.shape
    x2 = x.reshape(batch * seq, D)
    qkv = _qkv_proj(x2, w["w_in"], tables[0], tables[1], seq)
    hc = _glu_proj(x2, w["w_in"])
    attn_parts = [_dilated_attention(qkv, g, batch, seq) for g in range(N_GROUPS)]
    hcn = _conv_module(hc, w["conv_w"], w["conv_b"], w["conv_ln_g"], w["conv_ln_b"], batch, seq)
    attn = _attn_merge(attn_parts)
    x1, x1t = _merge_layer(x2, attn, hcn, w["w_in"], w["w_attn_o"], w["w_conv_o"], w["w_out"],
                           w["ln1_g"], w["ln1_b"])
    route = _peer_route(x1t, w["wq_t"], w["keys"])
    y = _peer_experts(x1t, x1, w["u"], w["v_t"], route, w["ln2_g"], w["ln2_b"])
    return y.reshape(batch, seq, D)


def kernel(x_prompt, x_sample, w_in, w_attn_o, conv_w, conv_b, conv_ln_g, conv_ln_b, w_conv_o, w_out,
           ln1_g, ln1_b, peer_w_q, peer_sub_keys, peer_u, peer_v, ln2_g, ln2_b):
    assert w_in.shape[0] == DEPTH == 1
    w = {
        "w_in": w_in[0].astype(BF16),
        "w_attn_o": w_attn_o[0].astype(BF16),
        "w_conv_o": w_conv_o[0].astype(BF16),
        "w_out": w_out[0].astype(BF16),
        "conv_w": conv_w[0], "conv_b": conv_b[0], "conv_ln_g": conv_ln_g[0], "conv_ln_b": conv_ln_b[0],
        "ln1_g": ln1_g[0], "ln1_b": ln1_b[0], "ln2_g": ln2_g[0], "ln2_b": ln2_b[0],
        "wq_t": peer_w_q[0].T.astype(BF16),
        "keys": peer_sub_keys[0].astype(BF16),
        "u": peer_u[0].astype(BF16),
        "v_t": peer_v[0].T.astype(BF16),
    }
    tables = _rope_tables(x_prompt.shape[1])
    assert x_sample.shape[1] == x_prompt.shape[1]
    return (_encoder_layer(x_prompt, w, tables), _encoder_layer(x_sample, w, tables))
```

```python
import functools
import math

import jax
import jax.numpy as jnp
from jax import lax
from jax.experimental import pallas as pl
from jax.experimental.pallas import tpu as pltpu

D = 2048
HEAD_DIM = 64
HEADS = 8
GROUP_W = HEADS * HEAD_DIM
DILATED_GROUPS = ((128, 1), (512, 4), (2048, 16))
N_GROUPS = len(DILATED_GROUPS)
ATTN_W = N_GROUPS * GROUP_W
BLK = 64
ROPE_THETA = 10000.0
CONV_C = D // 2
CONV_K = 31
IN_COLS = 3 * ATTN_W + 2 * CONV_C + 2 * D
GLU_COL0 = 3 * ATTN_W
GATE_COL0 = GLU_COL0 + 2 * CONV_C
PEER_HEADS = 8
N_KEYS = 128
N_EXPERTS = N_KEYS * N_KEYS
TOPK = 16
LN_EPS = 1e-5
DEPTH = 1
ALPHA = (2.0 * DEPTH) ** 0.25

LANES = 128
SUBLANES = 8
BF16_ROWS = 2 * SUBLANES
GATE_ROWS_LIVE = 2
DOT_KEYS = 4
ACC_ROWS = 512
VMEM_LIMIT = 56 * 1024 * 1024

F32 = jnp.float32
BF16 = jnp.bfloat16
NEG_INF = float("-inf")


def _params(semantics):
    return pltpu.CompilerParams(dimension_semantics=semantics, vmem_limit_bytes=VMEM_LIMIT)


def _layer_norm(y, g, b):
    mu = jnp.mean(y, axis=-1, keepdims=True)
    yc = y - mu
    var = jnp.mean(yc * yc, axis=-1, keepdims=True)
    return yc * lax.rsqrt(var + LN_EPS) * g + b


def _qkv_body(x_ref, w_ref, cos_ref, sin_ref, o0_ref, o1_ref, o2_ref, xb_ref, res_ref):
    j = pl.program_id(1)

    @pl.when(j == 0)
    def _():
        xb_ref[...] = x_ref[...].astype(BF16)

    acc = jnp.dot(xb_ref[...], w_ref[...], preferred_element_type=F32)

    @pl.when(j == 2)
    def _():
        for cb in range(ATTN_W // LANES):
            res_ref[cb] = acc[:, cb * LANES:(cb + 1) * LANES]

    @pl.when(j < 2)
    def _():
        scale = jnp.where(j == 0, 1.0 / math.sqrt(HEAD_DIM), 1.0).astype(F32)
        c = cos_ref[...] * scale
        s = sin_ref[...] * scale
        lane = lax.broadcasted_iota(jnp.int32, c.shape, 1)
        first_half = (lane % HEAD_DIM) < (HEAD_DIM // 2)
        for cb in range(ATTN_W // LANES):
            xc = acc[:, cb * LANES:(cb + 1) * LANES]
            partner = jnp.where(first_half,
                                pltpu.roll(xc, LANES - HEAD_DIM // 2, 1),
                                pltpu.roll(xc, HEAD_DIM // 2, 1))
            res_ref[cb] = xc * c + partner * s

    tm = res_ref.shape[1]
    per_group = GROUP_W // LANES
    for g, o_ref in enumerate((o0_ref, o1_ref, o2_ref)):
        d = DILATED_GROUPS[g][1]
        for r in range(d):
            for cb in range(per_group):
                rows = res_ref[g * per_group + cb, pl.ds(r, tm // d, stride=d), :]
                o_ref[r, :, cb * LANES:(cb + 1) * LANES] = rows.astype(o_ref.dtype)


def _qkv_proj(x2, w_in_bf, cos_t, sin_t, batch, seq):
    t = x2.shape[0]
    tm = 512
    nseq = seq // tm
    out_specs, out_shape = [], []
    for _, d in DILATED_GROUPS:
        assert tm % (d * BF16_ROWS) == 0
        out_specs.append(pl.BlockSpec((None, d, tm // d, GROUP_W), lambda i, j: (i // nseq, 0, i % nseq, j)))
        out_shape.append(jax.ShapeDtypeStruct((batch, d, seq // d, 3 * GROUP_W), BF16))
    return pl.pallas_call(
        _qkv_body,
        grid=(t // tm, 3),
        in_specs=[
            pl.BlockSpec((tm, D), lambda i, j: (i, 0)),
            pl.BlockSpec((D, ATTN_W), lambda i, j: (0, j)),
            pl.BlockSpec((tm, LANES), lambda i, j: (i % nseq, 0)),
            pl.BlockSpec((tm, LANES), lambda i, j: (i % nseq, 0)),
        ],
        out_specs=out_specs,
        out_shape=out_shape,
        scratch_shapes=[pltpu.VMEM((tm, D), BF16), pltpu.VMEM((ATTN_W // LANES, tm, LANES), F32)],
        compiler_params=_params(("parallel", "arbitrary")),
        name="qkv_proj",
    )(x2, w_in_bf, cos_t, sin_t)


def _glu_body(x_ref, wa_ref, wb_ref, o_ref, xb_ref):
    @pl.when(pl.program_id(1) == 0)
    def _():
        xb_ref[...] = x_ref[...].astype(BF16)

    xb = xb_ref[...]
    a = jnp.dot(xb, wa_ref[...], preferred_element_type=F32)
    b = jnp.dot(xb, wb_ref[...], preferred_element_type=F32)
    o_ref[...] = a * jax.nn.sigmoid(b)


def _glu_proj(x2, w_in_bf):
    t = x2.shape[0]
    tm, tn = 512, 512
    a0 = GLU_COL0 // tn
    b0 = (GLU_COL0 + CONV_C) // tn
    return pl.pallas_call(
        _glu_body,
        grid=(t // tm, CONV_C // tn),
        in_specs=[
            pl.BlockSpec((tm, D), lambda i, j: (i, 0)),
            pl.BlockSpec((D, tn), lambda i, j: (0, a0 + j)),
            pl.BlockSpec((D, tn), lambda i, j: (0, b0 + j)),
        ],
        out_specs=pl.BlockSpec((tm, tn), lambda i, j: (i, j)),
        out_shape=jax.ShapeDtypeStruct((t, CONV_C), F32),
        scratch_shapes=[pltpu.VMEM((tm, D), BF16)],
        compiler_params=_params(("parallel", "arbitrary")),
        name="glu_proj",
    )(x2, w_in_bf, w_in_bf)


def _attn_body(q_ref, kp_ref, km_ref, kn_ref, vp_ref, vm_ref, vn_ref, o_ref, lse_ref, *, sub_len):
    n = pl.program_id(2)
    tq = q_ref.shape[0]
    tk = tq + 2 * BLK
    q = q_ref[...]
    k = jnp.concatenate([kp_ref[...], km_ref[...], kn_ref[...]], axis=0)
    v = jnp.concatenate([vp_ref[...], vm_ref[...], vn_ref[...]], axis=0)
    row = lax.broadcasted_iota(jnp.int32, (tq, tk), 0)
    col = lax.broadcasted_iota(jnp.int32, (tq, tk), 1)
    pos = n * tq - BLK + col
    valid = (col >= row) & (col <= row + 2 * BLK) & (pos >= 0) & (pos < sub_len)
    for h in range(HEADS):
        sl = slice(h * HEAD_DIM, (h + 1) * HEAD_DIM)
        s = lax.dot_general(q[:, sl], k[:, sl], (((1,), (1,)), ((), ())), preferred_element_type=F32)
        s = jnp.where(valid, s, NEG_INF)
        m = jnp.max(s, axis=-1, keepdims=True)
        p = jnp.exp(s - m)
        l = jnp.sum(p, axis=-1, keepdims=True)
        o = jnp.dot(p.astype(BF16), v[:, sl], preferred_element_type=F32)
        o_ref[:, sl] = o / l
        lse_ref[:, sl] = jnp.broadcast_to(m + jnp.log(l), (tq, HEAD_DIM))


def _dilated_attention(qkv_g, g):
    window, d = DILATED_GROUPS[g]
    batch, _, sub_len, _ = qkv_g.shape
    assert window // (2 * d) == BLK and sub_len % BLK == 0
    tq = min(256, sub_len)
    nb = sub_len // BLK

    def main(col):
        return pl.BlockSpec((None, None, tq, GROUP_W), lambda b, r, n: (b, r, n, col))

    def prev(col):
        return pl.BlockSpec((None, None, BLK, GROUP_W),
                            lambda b, r, n: (b, r, jnp.maximum(n * (tq // BLK) - 1, 0), col))

    def nxt(col):
        return pl.BlockSpec((None, None, BLK, GROUP_W),
                            lambda b, r, n: (b, r, jnp.minimum((n + 1) * (tq // BLK), nb - 1), col))

    return pl.pallas_call(
        functools.partial(_attn_body, sub_len=sub_len),
        grid=(batch, d, sub_len // tq),
        in_specs=[main(0), prev(1), main(1), nxt(1), prev(2), main(2), nxt(2)],
        out_specs=[main(0), main(0)],
        out_shape=[jax.ShapeDtypeStruct((batch, d, sub_len, GROUP_W), F32)] * 2,
        compiler_params=_params(("parallel", "parallel", "arbitrary")),
        name=f"dilated_attn_{g}",
    )(qkv_g, qkv_g, qkv_g, qkv_g, qkv_g, qkv_g, qkv_g)


CONV_HALO = 16


def _conv_body(hp_ref, hm_ref, hn_ref, w_ref, b_ref, g_ref, beta_ref, o_ref, win_ref, y_ref):
    n = pl.program_id(1)
    tc = hm_ref.shape[0]
    win_ref[0:CONV_HALO, :] = jnp.where(n > 0, hp_ref[...], 0.0)
    win_ref[CONV_HALO:CONV_HALO + tc, :] = hm_ref[...]
    win_ref[CONV_HALO + tc:, :] = jnp.where(n < pl.num_programs(1) - 1, hn_ref[...], 0.0)
    off = CONV_HALO - CONV_K // 2
    for cb in range(CONV_C // LANES):
        cs = slice(cb * LANES, (cb + 1) * LANES)
        acc = jnp.zeros((tc, LANES), F32)
        for j in range(CONV_K):
            acc = acc + win_ref[off + j:off + j + tc, cs] * w_ref[j:j + 1, cs]
        y_ref[:, cs] = acc
    y = _layer_norm(y_ref[...] + b_ref[...], g_ref[...], beta_ref[...])
    o_ref[...] = (y * jax.nn.sigmoid(y)).astype(o_ref.dtype)


def _conv_module(hc, conv_w, conv_b, ln_g, ln_b, batch, seq):
    tc = 256
    nh = seq // CONV_HALO
    hc3 = hc.reshape(batch, seq, CONV_C)
    vec = pl.BlockSpec((1, CONV_C), lambda b, n: (0, 0))
    out = pl.pallas_call(
        _conv_body,
        grid=(batch, seq // tc),
        in_specs=[
            pl.BlockSpec((None, CONV_HALO, CONV_C),
                         lambda b, n: (b, jnp.maximum(n * (tc // CONV_HALO) - 1, 0), 0)),
            pl.BlockSpec((None, tc, CONV_C), lambda b, n: (b, n, 0)),
            pl.BlockSpec((None, CONV_HALO, CONV_C),
                         lambda b, n: (b, jnp.minimum((n + 1) * (tc // CONV_HALO), nh - 1), 0)),
            pl.BlockSpec((CONV_K, CONV_C), lambda b, n: (0, 0)),
            vec, vec, vec,
        ],
        out_specs=pl.BlockSpec((None, tc, CONV_C), lambda b, n: (b, n, 0)),
        out_shape=jax.ShapeDtypeStruct((batch, seq, CONV_C), BF16),
        scratch_shapes=[pltpu.VMEM((tc + 2 * CONV_HALO, CONV_C), F32), pltpu.VMEM((tc, CONV_C), F32)],
        compiler_params=_params(("parallel", "arbitrary")),
        name="conv_module",
    )(hc3, hc3, hc3, conv_w, conv_b.reshape(1, -1), ln_g.reshape(1, -1), ln_b.reshape(1, -1))
    return out.reshape(batch * seq, CONV_C)


def _attn_merge_body(o0_ref, l0_ref, o1_ref, l1_ref, o2_ref, l2_ref, attn_ref, *scratch):
    tm = attn_ref.shape[0]
    for cb in range(GROUP_W // LANES):
        cs = slice(cb * LANES, (cb + 1) * LANES)
        parts = [(o0_ref[0, :, cs], l0_ref[0, :, cs])]
        for g, (o_ref, l_ref) in enumerate(((o1_ref, l1_ref), (o2_ref, l2_ref)), start=1):
            d = DILATED_GROUPS[g][1]
            o_s, l_s = scratch[2 * (g - 1)], scratch[2 * (g - 1) + 1]
            for r in range(d):
                o_s[pl.ds(r, tm // d, stride=d), :] = o_ref[r, :, cs]
                l_s[pl.ds(r, tm // d, stride=d), :] = l_ref[r, :, cs]
            parts.append((o_s[...], l_s[...]))
        (o0, l0), (o1, l1), (o2, l2) = parts
        m = jnp.maximum(jnp.maximum(l0, l1), l2)
        e0, e1, e2 = jnp.exp(l0 - m), jnp.exp(l1 - m), jnp.exp(l2 - m)
        attn = (e0 * o0 + e1 * o1 + e2 * o2) / (e0 + e1 + e2)
        attn_ref[:, cs] = attn.astype(attn_ref.dtype)


def _attn_merge(attn_parts, batch, seq):
    tm = 512
    in_specs, args = [], []
    for g, (o, l) in enumerate(attn_parts):
        d = DILATED_GROUPS[g][1]
        assert tm % (d * SUBLANES) == 0
        spec = pl.BlockSpec((None, d, tm // d, GROUP_W), lambda b, n: (b, 0, n, 0))
        in_specs += [spec, spec]
        args += [o, l]
    attn = pl.pallas_call(
        _attn_merge_body,
        grid=(batch, seq // tm),
        in_specs=in_specs,
        out_specs=pl.BlockSpec((None, tm, GROUP_W), lambda b, n: (b, n, 0)),
        out_shape=jax.ShapeDtypeStruct((batch, seq, GROUP_W), BF16),
        scratch_shapes=[pltpu.VMEM((tm, LANES), F32)] * 4,
        compiler_params=_params(("parallel", "parallel")),
        name="attn_merge",
    )(*args)
    return attn.reshape(batch * seq, GROUP_W)


def _merge_body(x_ref, attn_ref, hcn_ref, wga_ref, wgb_ref, wao_ref, wco_ref, wout_ref, g_ref, b_ref,
                x1_ref, x1t_ref, xb_ref, acc_ref):
    j = pl.program_id(1)

    @pl.when(j == 0)
    def _():
        xb_ref[...] = x_ref[...].astype(BF16)
        acc_ref[...] = jnp.zeros_like(acc_ref)

    xb = xb_ref[...]
    ga = jax.nn.sigmoid(jnp.dot(xb, wga_ref[...], preferred_element_type=F32))
    gb = jax.nn.sigmoid(jnp.dot(xb, wgb_ref[...], preferred_element_type=F32))
    br_a = jnp.dot(attn_ref[...], wao_ref[...], preferred_element_type=F32)
    br_b = jnp.dot(hcn_ref[...], wco_ref[...], preferred_element_type=F32)
    merged = (ga * br_a + gb * br_b).astype(BF16)
    acc_ref[...] += jnp.dot(merged, wout_ref[...], preferred_element_type=F32)

    @pl.when(j == pl.num_programs(1) - 1)
    def _():
        x1 = _layer_norm(ALPHA * x_ref[...] + acc_ref[...], g_ref[...], b_ref[...])
        x1_ref[...] = x1
        x1t_ref[...] = x1.T.astype(BF16)


def _merge_layer(x2, attn, hcn, w_in_bf, w_ao_bf, w_co_bf, w_out_bf, ln_g, ln_b):
    t = x2.shape[0]
    tm, tn = 512, 512
    nj = D // tn
    ga0 = GATE_COL0 // tn
    gb0 = (GATE_COL0 + D) // tn
    vec = pl.BlockSpec((1, D), lambda i, j: (0, 0))
    return pl.pallas_call(
        _merge_body,
        grid=(t // tm, nj),
        in_specs=[
            pl.BlockSpec((tm, D), lambda i, j: (i, 0)),
            pl.BlockSpec((tm, GROUP_W), lambda i, j: (i, 0)),
            pl.BlockSpec((tm, CONV_C), lambda i, j: (i, 0)),
            pl.BlockSpec((D, tn), lambda i, j: (0, ga0 + j)),
            pl.BlockSpec((D, tn), lambda i, j: (0, gb0 + j)),
            pl.BlockSpec((GROUP_W, tn), lambda i, j: (0, j)),
            pl.BlockSpec((CONV_C, tn), lambda i, j: (0, j)),
            pl.BlockSpec((tn, D), lambda i, j: (j, 0)),
            vec, vec,
        ],
        out_specs=[pl.BlockSpec((tm, D), lambda i, j: (i, 0)),
                   pl.BlockSpec((D, tm), lambda i, j: (0, i))],
        out_shape=[jax.ShapeDtypeStruct((t, D), F32), jax.ShapeDtypeStruct((D, t), BF16)],
        scratch_shapes=[pltpu.VMEM((tm, D), BF16), pltpu.VMEM((tm, D), F32)],
        compiler_params=_params(("parallel", "arbitrary")),
        name="merge_layer",
    )(x2, attn, hcn, w_in_bf, w_in_bf, w_ao_bf, w_co_bf, w_out_bf, ln_g.reshape(1, -1), ln_b.reshape(1, -1))


RANK_CODE_BASE = -(2.0 ** 120)


def _top16(s, exact_ties):
    row = lax.broadcasted_iota(jnp.int32, s.shape, 0).astype(F32)
    vals = []
    for k in range(TOPK):
        m = jnp.max(s, axis=0, keepdims=True)
        hit = s == m
        if exact_ties:
            hit = row == jnp.min(jnp.where(hit, row, float(N_KEYS)), axis=0, keepdims=True)
        s = jnp.where(hit, RANK_CODE_BASE * (1.0 + k / TOPK), s)
        vals.append(m)
    taken = s <= RANK_CODE_BASE
    rank = jnp.where(taken, (s * (1.0 / RANK_CODE_BASE) - 1.0) * TOPK, float(TOPK))
    n_taken = jnp.sum(jnp.where(taken, 1.0, 0.0), axis=0, keepdims=True)
    return rank, jnp.concatenate(vals, axis=0), n_taken


def _pair_counts(a1, a2, exact_ties):
    half = TOPK // 2
    top = a1[0:1, :] + a2[0:1, :]
    blocks = [a1 + a2[0:1, :]]
    rows_a = [lax.broadcasted_iota(jnp.int32, (TOPK, LANES), 0)]
    rows_b = [jnp.zeros((TOPK, LANES), jnp.int32)]
    for b in range(1, TOPK):
        blocks.append(a1[0:half, :] + a2[b:b + 1, :])
        rows_a.append(lax.broadcasted_iota(jnp.int32, (half, LANES), 0))
        rows_b.append(jnp.full((half, LANES), b, jnp.int32))
    ra = jnp.concatenate(rows_a, axis=0)
    rb = jnp.concatenate(rows_b, axis=0)
    cand0 = jnp.where((ra + 1) * (rb + 1) <= TOPK, jnp.concatenate(blocks, axis=0), NEG_INF)
    pos = (ra * TOPK + rb).astype(F32)
    cand = cand0
    for _ in range(TOPK):
        m = jnp.max(cand, axis=0, keepdims=True)
        hit = cand == m
        if exact_ties:
            hit = pos == jnp.min(jnp.where(hit, pos, float(TOPK * TOPK)), axis=0, keepdims=True)
        cand = jnp.where(hit, NEG_INF, cand)
    chosen = cand != cand0
    cnt = jnp.where(chosen, 1.0, 0.0)
    n_lo = cnt[0:half, :]
    for b in range(1, TOPK):
        n_lo = n_lo + cnt[TOPK + (b - 1) * half:TOPK + b * half, :]
    n = jnp.concatenate([n_lo, cnt[half:TOPK, :]], axis=0)
    z = jnp.sum(jnp.where(chosen, jnp.exp(cand0 - top), 0.0), axis=0, keepdims=True)
    return n, z, jnp.sum(n, axis=0, keepdims=True)


def _route_body(x1t_ref, wq_ref, keys_ref, r2_ref, e2_ref, n1_ref, e1_ref, q_ref, sc_ref):
    tm = x1t_ref.shape[1]
    ns = tm // LANES
    q_ref[...] = jnp.dot(wq_ref[...], x1t_ref[...], preferred_element_type=F32).astype(BF16)
    for hp in range(2 * PEER_HEADS):
        st = jnp.dot(keys_ref[hp % 2], q_ref[hp * N_KEYS:(hp + 1) * N_KEYS, :],
                     preferred_element_type=F32)
        for sl in range(ns):
            sc_ref[hp // 2, sl, hp % 2] = st[:, sl * LANES:(sl + 1) * LANES]

    def route(exact_ties):
        def per_head_slab(idx, n_bad):
            h = idx // ns
            sl = idx % ns
            s1 = sc_ref[h, sl, 0]
            s2 = sc_ref[h, sl, 1]
            r1, a1, t1 = _top16(s1, exact_ties)
            r2, a2, t2 = _top16(s2, exact_ties)
            n, z, tc = _pair_counts(a1, a2, exact_ties)
            n1 = jnp.zeros_like(r1)
            for a in range(TOPK):
                n1 = jnp.where(r1 == float(a), n[a:a + 1, :], n1)
            e1 = jnp.where(r1 < float(TOPK), jnp.exp(s1 - a1[0:1, :]), 0.0)
            e2 = jnp.where(r2 < float(TOPK), jnp.exp(s2 - a2[0:1, :]), 0.0) / z
            r2_ref[sl, h] = r2.astype(r2_ref.dtype)
            e2_ref[sl, h] = e2.astype(e2_ref.dtype)
            n1_ref[sl, h] = n1
            e1_ref[sl, h] = e1
            ok = (t1 == float(TOPK)) & (t2 == float(TOPK)) & (tc == float(TOPK))
            return n_bad + jnp.where(ok, 0.0, 1.0)
        return per_head_slab

    n_bad = lax.fori_loop(0, PEER_HEADS * ns, route(False), jnp.zeros((1, LANES), F32))

    @pl.when(jnp.max(n_bad) > 0.0)
    def _():
        lax.fori_loop(0, PEER_HEADS * ns, route(True), jnp.zeros((1, LANES), F32))


def _peer_route(x1t, wq_t_bf, keys_bf):
    t = x1t.shape[1]
    tm = 512
    ns = tm // LANES
    blk = pl.BlockSpec((ns, PEER_HEADS, N_KEYS, LANES), lambda i: (i, 0, 0, 0))
    shape = (t // LANES, PEER_HEADS, N_KEYS, LANES)
    return pl.pallas_call(
        _route_body,
        grid=(t // tm,),
        in_specs=[
            pl.BlockSpec((D, tm), lambda i: (0, i)),
            pl.BlockSpec((D, D), lambda i: (0, 0)),
            pl.BlockSpec((2, N_KEYS, N_KEYS), lambda i: (0, 0, 0)),
        ],
        out_specs=[blk, blk, blk, blk],
        out_shape=[jax.ShapeDtypeStruct(shape, BF16), jax.ShapeDtypeStruct(shape, BF16),
                   jax.ShapeDtypeStruct(shape, F32), jax.ShapeDtypeStruct(shape, F32)],
        scratch_shapes=[pltpu.VMEM((D, tm), BF16), pltpu.VMEM((PEER_HEADS, ns, 2, N_KEYS, LANES), F32)],
        compiler_params=_params(("parallel",)),
        name="peer_route",
    )(x1t, wq_t_bf, keys_bf)


def _experts_body(x1t_ref, x1_ref, u_ref, vt_ref, r2_ref, e2_ref, n1_ref, e1_ref, g_ref, b_ref,
                  o_ref, acc_ref, gh_ref):
    c = pl.program_id(1)
    tm = x1t_ref.shape[1]
    ns = tm // LANES
    ni = u_ref.shape[0] // N_KEYS

    @pl.when(c == 0)
    def _():
        acc_ref[...] = jnp.zeros_like(acc_ref)

    nb = N_KEYS // BF16_ROWS
    zero = jnp.zeros((BF16_ROWS, LANES), BF16)
    for i0 in range(0, ni, GATE_ROWS_LIVE):
        for ts in range(ns):
            cs = slice(ts * LANES, (ts + 1) * LANES)
            gate = [[None] * nb for _ in range(GATE_ROWS_LIVE)]
            for h in range(PEER_HEADS):
                r2 = [r2_ref[ts, h, jb * BF16_ROWS:(jb + 1) * BF16_ROWS, :] for jb in range(nb)]
                e2 = [e2_ref[ts, h, jb * BF16_ROWS:(jb + 1) * BF16_ROWS, :] for jb in range(nb)]
                for ii in range(GATE_ROWS_LIVE):
                    i = i0 + ii
                    n1 = jnp.broadcast_to(n1_ref[ts, h, i:i + 1, :], (BF16_ROWS, LANES)).astype(BF16)
                    e1 = jnp.broadcast_to(e1_ref[ts, h, i:i + 1, :], (BF16_ROWS, LANES)).astype(BF16)
                    for jb in range(nb):
                        term = jnp.minimum(jnp.maximum(n1 - r2[jb], zero), e1) * e2[jb]
                        gate[ii][jb] = term if h == 0 else gate[ii][jb] + term
            for ii in range(GATE_ROWS_LIVE):
                for jb in range(nb):
                    r0 = (i0 + ii) * N_KEYS + jb * BF16_ROWS
                    gh_ref[r0:r0 + BF16_ROWS, cs] = gate[ii][jb]
        if (i0 + GATE_ROWS_LIVE) % DOT_KEYS == 0:
            rows = slice((i0 + GATE_ROWS_LIVE - DOT_KEYS) * N_KEYS, (i0 + GATE_ROWS_LIVE) * N_KEYS)
            ht = jnp.dot(u_ref[rows, :], x1t_ref[...], preferred_element_type=F32)
            act = (0.5 * ht * (1.0 + lax.erf(ht * (1.0 / math.sqrt(2.0))))).astype(BF16)
            gh_ref[rows, :] = gh_ref[rows, :] * act
    for r0 in range(0, D, ACC_ROWS):
        acc_ref[r0:r0 + ACC_ROWS, :] += jnp.dot(vt_ref[r0:r0 + ACC_ROWS, :], gh_ref[...],
                                                preferred_element_type=F32)

    @pl.when(c == pl.num_programs(1) - 1)
    def _():
        y = ALPHA * x1_ref[...] + acc_ref[...].T
        o_ref[...] = _layer_norm(y, g_ref[...], b_ref[...])


def _peer_experts(x1t, x1, u_bf, vt_bf, route, ln_g, ln_b):
    t = x1.shape[0]
    tm, ec = 512, 1024
    ns = tm // LANES
    r2, e2, n1, e1 = route
    ni = ec // N_KEYS
    blk = pl.BlockSpec((ns, PEER_HEADS, N_KEYS, LANES), lambda i, c: (i, 0, 0, 0))
    row = pl.BlockSpec((ns, PEER_HEADS, ni, LANES), lambda i, c: (i, 0, c, 0))
    vec = pl.BlockSpec((1, D), lambda i, c: (0, 0))
    return pl.pallas_call(
        _experts_body,
        grid=(t // tm, N_EXPERTS // ec),
        in_specs=[
            pl.BlockSpec((D, tm), lambda i, c: (0, i)),
            pl.BlockSpec((tm, D), lambda i, c: (i, 0), pipeline_mode=pl.Buffered(1)),
            pl.BlockSpec((ec, D), lambda i, c: (c, 0)),
            pl.BlockSpec((D, ec), lambda i, c: (0, c)),
            blk, blk, row, row, vec, vec,
        ],
        out_specs=pl.BlockSpec((tm, D), lambda i, c: (i, 0)),
        out_shape=jax.ShapeDtypeStruct((t, D), F32),
        scratch_shapes=[pltpu.VMEM((D, tm), F32), pltpu.VMEM((ec, tm), BF16)],
        compiler_params=_params(("parallel", "arbitrary")),
        name="peer_experts",
    )(x1t, x1, u_bf, vt_bf, r2, e2, n1, e1, ln_g.reshape(1, -1), ln_b.reshape(1, -1))


def _rope_tables(seq):
    half = HEAD_DIM // 2
    inv_freq = ROPE_THETA ** (-jnp.arange(half, dtype=F32) / half)
    ang = jnp.arange(seq, dtype=F32)[:, None] * inv_freq[None, :]
    cos, sin = jnp.cos(ang), jnp.sin(ang)
    reps = LANES // HEAD_DIM
    cos_t = jnp.tile(jnp.concatenate([cos, cos], axis=1), (1, reps))
    sin_t = jnp.tile(jnp.concatenate([-sin, sin], axis=1), (1, reps))
    return cos_t, sin_t


def _encoder_layer(x, w, tables):
    batch, seq, _ = x.shape
    x2 = x.reshape(batch * seq, D)
    qkv = _qkv_proj(x2, w["w_in"], tables[0], tables[1], batch, seq)
    hc = _glu_proj(x2, w["w_in"])
    attn_parts = [_dilated_attention(qkv[g], g) for g in range(N_GROUPS)]
    hcn = _conv_module(hc, w["conv_w"], w["conv_b"], w["conv_ln_g"], w["conv_ln_b"], batch, seq)
    attn = _attn_merge(attn_parts, batch, seq)
    x1, x1t = _merge_layer(x2, attn, hcn, w["w_in"], w["w_attn_o"], w["w_conv_o"], w["w_out"],
                           w["ln1_g"], w["ln1_b"])
    route = _peer_route(x1t, w["wq_t"], w["keys"])
    y = _peer_experts(x1t, x1, w["u"], w["v_t"], route, w["ln2_g"], w["ln2_b"])
    return y.reshape(batch, seq, D)


def kernel(x_prompt, x_sample, w_in, w_attn_o, conv_w, conv_b, conv_ln_g, conv_ln_b, w_conv_o, w_out,
           ln1_g, ln1_b, peer_w_q, peer_sub_keys, peer_u, peer_v, ln2_g, ln2_b):
    assert w_in.shape[0] == DEPTH == 1
    w = {
        "w_in": w_in[0].astype(BF16),
        "w_attn_o": w_attn_o[0].astype(BF16),
        "w_conv_o": w_conv_o[0].astype(BF16),
        "w_out": w_out[0].astype(BF16),
        "conv_w": conv_w[0], "conv_b": conv_b[0], "conv_ln_g": conv_ln_g[0], "conv_ln_b": conv_ln_b[0],
        "ln1_g": ln1_g[0], "ln1_b": ln1_b[0], "ln2_g": ln2_g[0], "ln2_b": ln2_b[0],
        "wq_t": peer_w_q[0].T.astype(BF16),
        "keys": peer_sub_keys[0].astype(BF16),
        "u": peer_u[0].astype(BF16),
        "v_t": peer_v[0].T.astype(BF16),
    }
    tables = _rope_tables(x_prompt.shape[1])
    assert x_sample.shape[1] == x_prompt.shape[1]
    return (_encoder_layer(x_prompt, w, tables), _encoder_layer(x_sample, w, tables))
```

```python
import functools
import math

import jax
import jax.numpy as jnp
from jax import lax
from jax.experimental import pallas as pl
from jax.experimental.pallas import tpu as pltpu

D = 2048
HEAD_DIM = 64
HEADS = 8
GROUP_W = HEADS * HEAD_DIM
DILATED_GROUPS = ((128, 1), (512, 4), (2048, 16))
N_GROUPS = len(DILATED_GROUPS)
ATTN_W = N_GROUPS * GROUP_W
BLK = 64
ROPE_THETA = 10000.0
CONV_C = D // 2
CONV_K = 31
IN_COLS = 3 * ATTN_W + 2 * CONV_C + 2 * D
GLU_COL0 = 3 * ATTN_W
GATE_COL0 = GLU_COL0 + 2 * CONV_C
PEER_HEADS = 8
N_KEYS = 128
N_EXPERTS = N_KEYS * N_KEYS
TOPK = 16
LN_EPS = 1e-5
DEPTH = 1
ALPHA = (2.0 * DEPTH) ** 0.25

LANES = 128
SUBLANES = 8
BF16_ROWS = 2 * SUBLANES
GATE_ROWS_LIVE = 2
DOT_KEYS = 4
ACC_ROWS = 512
ATTN_ROWS = 128
VMEM_LIMIT = 56 * 1024 * 1024

F32 = jnp.float32
BF16 = jnp.bfloat16
NEG_INF = float("-inf")


def _params(semantics):
    return pltpu.CompilerParams(dimension_semantics=semantics, vmem_limit_bytes=VMEM_LIMIT)


def _layer_norm(y, g, b):
    mu = jnp.mean(y, axis=-1, keepdims=True)
    yc = y - mu
    var = jnp.mean(yc * yc, axis=-1, keepdims=True)
    return yc * lax.rsqrt(var + LN_EPS) * g + b


def _qkv_body(x_ref, w_ref, cos_ref, sin_ref, o0_ref, o1_ref, o2_ref, xb_ref, res_ref):
    j = pl.program_id(1)

    @pl.when(j == 0)
    def _():
        xb_ref[...] = x_ref[...].astype(BF16)

    acc = jnp.dot(xb_ref[...], w_ref[...], preferred_element_type=F32)

    @pl.when(j == 2)
    def _():
        for cb in range(ATTN_W // LANES):
            res_ref[cb] = acc[:, cb * LANES:(cb + 1) * LANES]

    @pl.when(j < 2)
    def _():
        scale = jnp.where(j == 0, 1.0 / math.sqrt(HEAD_DIM), 1.0).astype(F32)
        c = cos_ref[...] * scale
        s = sin_ref[...] * scale
        lane = lax.broadcasted_iota(jnp.int32, c.shape, 1)
        first_half = (lane % HEAD_DIM) < (HEAD_DIM // 2)
        for cb in range(ATTN_W // LANES):
            xc = acc[:, cb * LANES:(cb + 1) * LANES]
            partner = jnp.where(first_half,
                                pltpu.roll(xc, LANES - HEAD_DIM // 2, 1),
                                pltpu.roll(xc, HEAD_DIM // 2, 1))
            res_ref[cb] = xc * c + partner * s

    tm = res_ref.shape[1]
    per_group = GROUP_W // LANES
    for g, o_ref in enumerate((o0_ref, o1_ref, o2_ref)):
        d = DILATED_GROUPS[g][1]
        for r in range(d):
            for cb in range(per_group):
                rows = res_ref[g * per_group + cb, pl.ds(r, tm // d, stride=d), :]
                o_ref[r, :, cb * LANES:(cb + 1) * LANES] = rows.astype(o_ref.dtype)


def _qkv_proj(x2, w_in_bf, cos_t, sin_t, batch, seq):
    t = x2.shape[0]
    tm = 512
    nseq = seq // tm
    out_specs, out_shape = [], []
    for _, d in DILATED_GROUPS:
        assert tm % (d * BF16_ROWS) == 0
        out_specs.append(pl.BlockSpec((None, d, tm // d, GROUP_W), lambda i, j: (i // nseq, 0, i % nseq, j)))
        out_shape.append(jax.ShapeDtypeStruct((batch, d, seq // d, 3 * GROUP_W), BF16))
    return pl.pallas_call(
        _qkv_body,
        grid=(t // tm, 3),
        in_specs=[
            pl.BlockSpec((tm, D), lambda i, j: (i, 0)),
            pl.BlockSpec((D, ATTN_W), lambda i, j: (0, j)),
            pl.BlockSpec((tm, LANES), lambda i, j: (i % nseq, 0)),
            pl.BlockSpec((tm, LANES), lambda i, j: (i % nseq, 0)),
        ],
        out_specs=out_specs,
        out_shape=out_shape,
        scratch_shapes=[pltpu.VMEM((tm, D), BF16), pltpu.VMEM((ATTN_W // LANES, tm, LANES), F32)],
        compiler_params=_params(("parallel", "arbitrary")),
        name="qkv_proj",
    )(x2, w_in_bf, cos_t, sin_t)


def _glu_body(x_ref, wa_ref, wb_ref, o_ref, xb_ref):
    @pl.when(pl.program_id(1) == 0)
    def _():
        xb_ref[...] = x_ref[...].astype(BF16)

    xb = xb_ref[...]
    a = jnp.dot(xb, wa_ref[...], preferred_element_type=F32)
    b = jnp.dot(xb, wb_ref[...], preferred_element_type=F32)
    o_ref[...] = a * jax.nn.sigmoid(b)


def _glu_proj(x2, w_in_bf):
    t = x2.shape[0]
    tm, tn = 512, 512
    a0 = GLU_COL0 // tn
    b0 = (GLU_COL0 + CONV_C) // tn
    return pl.pallas_call(
        _glu_body,
        grid=(t // tm, CONV_C // tn),
        in_specs=[
            pl.BlockSpec((tm, D), lambda i, j: (i, 0)),
            pl.BlockSpec((D, tn), lambda i, j: (0, a0 + j)),
            pl.BlockSpec((D, tn), lambda i, j: (0, b0 + j)),
        ],
        out_specs=pl.BlockSpec((tm, tn), lambda i, j: (i, j)),
        out_shape=jax.ShapeDtypeStruct((t, CONV_C), F32),
        scratch_shapes=[pltpu.VMEM((tm, D), BF16)],
        compiler_params=_params(("parallel", "arbitrary")),
        name="glu_proj",
    )(x2, w_in_bf, w_in_bf)


def _attn_body(q_ref, kp_ref, km_ref, kn_ref, vp_ref, vm_ref, vn_ref, o_ref, lse_ref, *, sub_len):
    n = pl.program_id(2)
    tq = q_ref.shape[0]
    k = jnp.concatenate([kp_ref[...], km_ref[...], kn_ref[...]], axis=0)
    v = jnp.concatenate([vp_ref[...], vm_ref[...], vn_ref[...]], axis=0)
    tk = ATTN_ROWS + 2 * BLK
    row = lax.broadcasted_iota(jnp.int32, (ATTN_ROWS, tk), 0)
    col = lax.broadcasted_iota(jnp.int32, (ATTN_ROWS, tk), 1)
    band = (col >= row) & (col <= row + 2 * BLK)
    heads = [slice(h * HEAD_DIM, (h + 1) * HEAD_DIM) for h in range(HEADS)]
    for r0 in range(0, tq, ATTN_ROWS):
        pos = n * tq + r0 - BLK + col
        valid = band & (pos >= 0) & (pos < sub_len)
        scores = [lax.dot_general(q_ref[r0:r0 + ATTN_ROWS, sl], k[r0:r0 + tk, sl], (((1,), (1,)), ((), ())),
                                  preferred_element_type=F32) for sl in heads]
        probs, stats = [], []
        for s in scores:
            s = jnp.where(valid, s, NEG_INF)
            m = jnp.max(s, axis=-1, keepdims=True)
            p = jnp.exp(s - m)
            l = jnp.sum(p, axis=-1, keepdims=True)
            probs.append(p.astype(BF16))
            stats.append((m, l))
        outs = [jnp.dot(p, v[r0:r0 + tk, sl], preferred_element_type=F32) for p, sl in zip(probs, heads)]
        for sl, o, (m, l) in zip(heads, outs, stats):
            o_ref[r0:r0 + ATTN_ROWS, sl] = o / l
            lse_ref[r0:r0 + ATTN_ROWS, sl] = jnp.broadcast_to(m + jnp.log(l), (ATTN_ROWS, HEAD_DIM))


def _dilated_attention(qkv_g, g):
    window, d = DILATED_GROUPS[g]
    batch, _, sub_len, _ = qkv_g.shape
    assert window // (2 * d) == BLK and sub_len % BLK == 0
    tq = min(256, sub_len)
    nb = sub_len // BLK

    def main(col):
        return pl.BlockSpec((None, None, tq, GROUP_W), lambda b, r, n: (b, r, n, col))

    def prev(col):
        return pl.BlockSpec((None, None, BLK, GROUP_W),
                            lambda b, r, n: (b, r, jnp.maximum(n * (tq // BLK) - 1, 0), col))

    def nxt(col):
        return pl.BlockSpec((None, None, BLK, GROUP_W),
                            lambda b, r, n: (b, r, jnp.minimum((n + 1) * (tq // BLK), nb - 1), col))

    return pl.pallas_call(
        functools.partial(_attn_body, sub_len=sub_len),
        grid=(batch, d, sub_len // tq),
        in_specs=[main(0), prev(1), main(1), nxt(1), prev(2), main(2), nxt(2)],
        out_specs=[main(0), main(0)],
        out_shape=[jax.ShapeDtypeStruct((batch, d, sub_len, GROUP_W), F32)] * 2,
        compiler_params=_params(("parallel", "parallel", "arbitrary")),
        name=f"dilated_attn_{g}",
    )(qkv_g, qkv_g, qkv_g, qkv_g, qkv_g, qkv_g, qkv_g)


CONV_HALO = 16
CONV_ROWS = 64


def _conv_body(hp_ref, hm_ref, hn_ref, w_ref, b_ref, g_ref, beta_ref, o_ref, win_ref, sh_ref, y_ref):
    n = pl.program_id(1)
    tc = hm_ref.shape[0]
    win_ref[0:CONV_HALO, :] = jnp.where(n > 0, hp_ref[...], 0.0)
    win_ref[CONV_HALO:CONV_HALO + tc, :] = hm_ref[...]
    win_ref[CONV_HALO + tc:, :] = jnp.where(n < pl.num_programs(1) - 1, hn_ref[...], 0.0)
    off = CONV_HALO - CONV_K // 2
    span = tc + CONV_HALO + SUBLANES
    for b in range(1, SUBLANES):
        sh_ref[b - 1] = win_ref[b:b + span, :]
    for cb in range(CONV_C // LANES):
        cs = slice(cb * LANES, (cb + 1) * LANES)
        for r0 in range(0, tc, CONV_ROWS):
            acc = jnp.zeros((CONV_ROWS, LANES), F32)
            for j in range(CONV_K):
                a, b = divmod(off + j, SUBLANES)
                rows = slice(r0 + a * SUBLANES, r0 + a * SUBLANES + CONV_ROWS)
                tap = win_ref[rows, cs] if b == 0 else sh_ref[b - 1, rows, cs]
                acc = acc + tap * w_ref[j:j + 1, cs]
            y_ref[r0:r0 + CONV_ROWS, cs] = acc
    y = _layer_norm(y_ref[...] + b_ref[...], g_ref[...], beta_ref[...])
    o_ref[...] = (y * jax.nn.sigmoid(y)).astype(o_ref.dtype)


def _conv_module(hc, conv_w, conv_b, ln_g, ln_b, batch, seq):
    tc = 256
    nh = seq // CONV_HALO
    hc3 = hc.reshape(batch, seq, CONV_C)
    vec = pl.BlockSpec((1, CONV_C), lambda b, n: (0, 0))
    out = pl.pallas_call(
        _conv_body,
        grid=(batch, seq // tc),
        in_specs=[
            pl.BlockSpec((None, CONV_HALO, CONV_C),
                         lambda b, n: (b, jnp.maximum(n * (tc // CONV_HALO) - 1, 0), 0)),
            pl.BlockSpec((None, tc, CONV_C), lambda b, n: (b, n, 0)),
            pl.BlockSpec((None, CONV_HALO, CONV_C),
                         lambda b, n: (b, jnp.minimum((n + 1) * (tc // CONV_HALO), nh - 1), 0)),
            pl.BlockSpec((CONV_K, CONV_C), lambda b, n: (0, 0)),
            vec, vec, vec,
        ],
        out_specs=pl.BlockSpec((None, tc, CONV_C), lambda b, n: (b, n, 0)),
        out_shape=jax.ShapeDtypeStruct((batch, seq, CONV_C), BF16),
        scratch_shapes=[pltpu.VMEM((tc + 2 * CONV_HALO, CONV_C), F32),
                        pltpu.VMEM((SUBLANES - 1, tc + CONV_HALO + SUBLANES, CONV_C), F32),
                        pltpu.VMEM((tc, CONV_C), F32)],
        compiler_params=_params(("parallel", "arbitrary")),
        name="conv_module",
    )(hc3, hc3, hc3, conv_w, conv_b.reshape(1, -1), ln_g.reshape(1, -1), ln_b.reshape(1, -1))
    return out.reshape(batch * seq, CONV_C)


def _attn_merge_body(o0_ref, l0_ref, o1_ref, l1_ref, o2_ref, l2_ref, attn_ref, *scratch):
    tm = attn_ref.shape[0]
    for cb in range(GROUP_W // LANES):
        cs = slice(cb * LANES, (cb + 1) * LANES)
        parts = [(o0_ref[0, :, cs], l0_ref[0, :, cs])]
        for g, (o_ref, l_ref) in enumerate(((o1_ref, l1_ref), (o2_ref, l2_ref)), start=1):
            d = DILATED_GROUPS[g][1]
            o_s, l_s = scratch[2 * (g - 1)], scratch[2 * (g - 1) + 1]
            for r in range(d):
                o_s[pl.ds(r, tm // d, stride=d), :] = o_ref[r, :, cs]
                l_s[pl.ds(r, tm // d, stride=d), :] = l_ref[r, :, cs]
            parts.append((o_s[...], l_s[...]))
        (o0, l0), (o1, l1), (o2, l2) = parts
        m = jnp.maximum(jnp.maximum(l0, l1), l2)
        e0, e1, e2 = jnp.exp(l0 - m), jnp.exp(l1 - m), jnp.exp(l2 - m)
        attn = (e0 * o0 + e1 * o1 + e2 * o2) / (e0 + e1 + e2)
        attn_ref[:, cs] = attn.astype(attn_ref.dtype)


def _attn_merge(attn_parts, batch, seq):
    tm = 512
    in_specs, args = [], []
    for g, (o, l) in enumerate(attn_parts):
        d = DILATED_GROUPS[g][1]
        assert tm % (d * SUBLANES) == 0
        spec = pl.BlockSpec((None, d, tm // d, GROUP_W), lambda b, n: (b, 0, n, 0))
        in_specs += [spec, spec]
        args += [o, l]
    attn = pl.pallas_call(
        _attn_merge_body,
        grid=(batch, seq // tm),
        in_specs=in_specs,
        out_specs=pl.BlockSpec((None, tm, GROUP_W), lambda b, n: (b, n, 0)),
        out_shape=jax.ShapeDtypeStruct((batch, seq, GROUP_W), BF16),
        scratch_shapes=[pltpu.VMEM((tm, LANES), F32)] * 4,
        compiler_params=_params(("parallel", "parallel")),
        name="attn_merge",
    )(*args)
    return attn.reshape(batch * seq, GROUP_W)


def _merge_body(x_ref, attn_ref, hcn_ref, wga_ref, wgb_ref, wao_ref, wco_ref, wout_ref, g_ref, b_ref,
                x1_ref, x1t_ref, xb_ref, acc_ref):
    j = pl.program_id(1)

    @pl.when(j == 0)
    def _():
        xb_ref[...] = x_ref[...].astype(BF16)
        acc_ref[...] = jnp.zeros_like(acc_ref)

    xb = xb_ref[...]
    ga = jax.nn.sigmoid(jnp.dot(xb, wga_ref[...], preferred_element_type=F32))
    gb = jax.nn.sigmoid(jnp.dot(xb, wgb_ref[...], preferred_element_type=F32))
    br_a = jnp.dot(attn_ref[...], wao_ref[...], preferred_element_type=F32)
    br_b = jnp.dot(hcn_ref[...], wco_ref[...], preferred_element_type=F32)
    merged = (ga * br_a + gb * br_b).astype(BF16)
    acc_ref[...] += jnp.dot(merged, wout_ref[...], preferred_element_type=F32)

    @pl.when(j == pl.num_programs(1) - 1)
    def _():
        x1 = _layer_norm(ALPHA * x_ref[...] + acc_ref[...], g_ref[...], b_ref[...])
        x1_ref[...] = x1
        x1t_ref[...] = x1.T.astype(BF16)


def _merge_layer(x2, attn, hcn, w_in_bf, w_ao_bf, w_co_bf, w_out_bf, ln_g, ln_b):
    t = x2.shape[0]
    tm, tn = 512, 512
    nj = D // tn
    ga0 = GATE_COL0 // tn
    gb0 = (GATE_COL0 + D) // tn
    vec = pl.BlockSpec((1, D), lambda i, j: (0, 0))
    return pl.pallas_call(
        _merge_body,
        grid=(t // tm, nj),
        in_specs=[
            pl.BlockSpec((tm, D), lambda i, j: (i, 0)),
            pl.BlockSpec((tm, GROUP_W), lambda i, j: (i, 0)),
            pl.BlockSpec((tm, CONV_C), lambda i, j: (i, 0)),
            pl.BlockSpec((D, tn), lambda i, j: (0, ga0 + j)),
            pl.BlockSpec((D, tn), lambda i, j: (0, gb0 + j)),
            pl.BlockSpec((GROUP_W, tn), lambda i, j: (0, j)),
            pl.BlockSpec((CONV_C, tn), lambda i, j: (0, j)),
            pl.BlockSpec((tn, D), lambda i, j: (j, 0)),
            vec, vec,
        ],
        out_specs=[pl.BlockSpec((tm, D), lambda i, j: (i, 0)),
                   pl.BlockSpec((D, tm), lambda i, j: (0, i))],
        out_shape=[jax.ShapeDtypeStruct((t, D), F32), jax.ShapeDtypeStruct((D, t), BF16)],
        scratch_shapes=[pltpu.VMEM((tm, D), BF16), pltpu.VMEM((tm, D), F32)],
        compiler_params=_params(("parallel", "arbitrary")),
        name="merge_layer",
    )(x2, attn, hcn, w_in_bf, w_in_bf, w_ao_bf, w_co_bf, w_out_bf, ln_g.reshape(1, -1), ln_b.reshape(1, -1))


RANK_CODE_BASE = -(2.0 ** 120)


def _top16(s, exact_ties):
    row = lax.broadcasted_iota(jnp.int32, s.shape, 0).astype(F32)
    vals = []
    for k in range(TOPK):
        m = jnp.max(s, axis=0, keepdims=True)
        hit = s == m
        if exact_ties:
            hit = row == jnp.min(jnp.where(hit, row, float(N_KEYS)), axis=0, keepdims=True)
        s = jnp.where(hit, RANK_CODE_BASE * (1.0 + k / TOPK), s)
        vals.append(m)
    taken = s <= RANK_CODE_BASE
    rank = jnp.where(taken, (s * (1.0 / RANK_CODE_BASE) - 1.0) * TOPK, float(TOPK))
    n_taken = jnp.sum(jnp.where(taken, 1.0, 0.0), axis=0, keepdims=True)
    return rank, jnp.concatenate(vals, axis=0), n_taken


def _pair_counts(a1, a2, exact_ties):
    half = TOPK // 2
    top = a1[0:1, :] + a2[0:1, :]
    blocks = [a1 + a2[0:1, :]]
    rows_a = [lax.broadcasted_iota(jnp.int32, (TOPK, LANES), 0)]
    rows_b = [jnp.zeros((TOPK, LANES), jnp.int32)]
    for b in range(1, TOPK):
        blocks.append(a1[0:half, :] + a2[b:b + 1, :])
        rows_a.append(lax.broadcasted_iota(jnp.int32, (half, LANES), 0))
        rows_b.append(jnp.full((half, LANES), b, jnp.int32))
    ra = jnp.concatenate(rows_a, axis=0)
    rb = jnp.concatenate(rows_b, axis=0)
    cand0 = jnp.where((ra + 1) * (rb + 1) <= TOPK, jnp.concatenate(blocks, axis=0), NEG_INF)
    pos = (ra * TOPK + rb).astype(F32)
    cand = cand0
    for _ in range(TOPK):
        m = jnp.max(cand, axis=0, keepdims=True)
        hit = cand == m
        if exact_ties:
            hit = pos == jnp.min(jnp.where(hit, pos, float(TOPK * TOPK)), axis=0, keepdims=True)
        cand = jnp.where(hit, NEG_INF, cand)
    chosen = cand != cand0
    cnt = jnp.where(chosen, 1.0, 0.0)
    n_lo = cnt[0:half, :]
    for b in range(1, TOPK):
        n_lo = n_lo + cnt[TOPK + (b - 1) * half:TOPK + b * half, :]
    n = jnp.concatenate([n_lo, cnt[half:TOPK, :]], axis=0)
    z = jnp.sum(jnp.where(chosen, jnp.exp(cand0 - top), 0.0), axis=0, keepdims=True)
    return n, z, jnp.sum(n, axis=0, keepdims=True)


def _route_body(x1t_ref, wq_ref, keys_ref, r2_ref, e2_ref, n1_ref, e1_ref, q_ref, sc_ref):
    tm = x1t_ref.shape[1]
    ns = tm // LANES
    q_ref[...] = jnp.dot(wq_ref[...], x1t_ref[...], preferred_element_type=F32).astype(BF16)
    for hp in range(2 * PEER_HEADS):
        st = jnp.dot(keys_ref[hp % 2], q_ref[hp * N_KEYS:(hp + 1) * N_KEYS, :],
                     preferred_element_type=F32)
        for sl in range(ns):
            sc_ref[hp // 2, sl, hp % 2] = st[:, sl * LANES:(sl + 1) * LANES]

    def per_head_slab(idx, carry):
        h = idx // ns
        sl = idx % ns

        def route(exact_ties):
            s1 = sc_ref[h, sl, 0]
            s2 = sc_ref[h, sl, 1]
            r1, a1, t1 = _top16(s1, exact_ties)
            r2, a2, t2 = _top16(s2, exact_ties)
            n, z, tc = _pair_counts(a1, a2, exact_ties)
            n1 = jnp.zeros_like(r1)
            for a in range(TOPK):
                n1 = jnp.where(r1 == float(a), n[a:a + 1, :], n1)
            e1 = jnp.where(r1 < float(TOPK), jnp.exp(s1 - a1[0:1, :]), 0.0)
            e2 = jnp.where(r2 < float(TOPK), jnp.exp(s2 - a2[0:1, :]), 0.0) / z
            r2_ref[sl, h] = r2.astype(r2_ref.dtype)
            e2_ref[sl, h] = e2.astype(e2_ref.dtype)
            n1_ref[sl, h] = n1
            e1_ref[sl, h] = e1
            return (t1 == float(TOPK)) & (t2 == float(TOPK)) & (tc == float(TOPK))

        ok = route(False)

        @pl.when(jnp.max(jnp.where(ok, 0.0, 1.0)) > 0.0)
        def _():
            route(True)

        return carry

    lax.fori_loop(0, PEER_HEADS * ns, per_head_slab, 0)


def _peer_route(x1t, wq_t_bf, keys_bf):
    t = x1t.shape[1]
    tm = 512
    ns = tm // LANES
    blk = pl.BlockSpec((ns, PEER_HEADS, N_KEYS, LANES), lambda i: (i, 0, 0, 0))
    shape = (t // LANES, PEER_HEADS, N_KEYS, LANES)
    return pl.pallas_call(
        _route_body,
        grid=(t // tm,),
        in_specs=[
            pl.BlockSpec((D, tm), lambda i: (0, i)),
            pl.BlockSpec((D, D), lambda i: (0, 0)),
            pl.BlockSpec((2, N_KEYS, N_KEYS), lambda i: (0, 0, 0)),
        ],
        out_specs=[blk, blk, blk, blk],
        out_shape=[jax.ShapeDtypeStruct(shape, BF16), jax.ShapeDtypeStruct(shape, BF16),
                   jax.ShapeDtypeStruct(shape, F32), jax.ShapeDtypeStruct(shape, F32)],
        scratch_shapes=[pltpu.VMEM((D, tm), BF16), pltpu.VMEM((PEER_HEADS, ns, 2, N_KEYS, LANES), F32)],
        compiler_params=_params(("parallel",)),
        name="peer_route",
    )(x1t, wq_t_bf, keys_bf)


def _experts_body(x1t_ref, x1_ref, u_ref, vt_ref, r2_ref, e2_ref, n1_ref, e1_ref, g_ref, b_ref,
                  o_ref, acc_ref, gh_ref):
    c = pl.program_id(1)
    tm = x1t_ref.shape[1]
    ns = tm // LANES
    ni = u_ref.shape[0] // N_KEYS

    @pl.when(c == 0)
    def _():
        acc_ref[...] = jnp.zeros_like(acc_ref)

    nb = N_KEYS // BF16_ROWS
    zero = jnp.zeros((BF16_ROWS, LANES), BF16)
    for i0 in range(0, ni, GATE_ROWS_LIVE):
        for ts in range(ns):
            cs = slice(ts * LANES, (ts + 1) * LANES)
            gate = [[None] * nb for _ in range(GATE_ROWS_LIVE)]
            for h in range(PEER_HEADS):
                r2 = [r2_ref[ts, h, jb * BF16_ROWS:(jb + 1) * BF16_ROWS, :] for jb in range(nb)]
                e2 = [e2_ref[ts, h, jb * BF16_ROWS:(jb + 1) * BF16_ROWS, :] for jb in range(nb)]
                for ii in range(GATE_ROWS_LIVE):
                    i = i0 + ii
                    n1 = jnp.broadcast_to(n1_ref[ts, h, i:i + 1, :], (BF16_ROWS, LANES)).astype(BF16)
                    e1 = jnp.broadcast_to(e1_ref[ts, h, i:i + 1, :], (BF16_ROWS, LANES)).astype(BF16)
                    for jb in range(nb):
                        term = jnp.minimum(jnp.maximum(n1 - r2[jb], zero), e1) * e2[jb]
                        gate[ii][jb] = term if h == 0 else gate[ii][jb] + term
            for ii in range(GATE_ROWS_LIVE):
                for jb in range(nb):
                    r0 = (i0 + ii) * N_KEYS + jb * BF16_ROWS
                    gh_ref[r0:r0 + BF16_ROWS, cs] = gate[ii][jb]
        if (i0 + GATE_ROWS_LIVE) % DOT_KEYS == 0:
            rows = slice((i0 + GATE_ROWS_LIVE - DOT_KEYS) * N_KEYS, (i0 + GATE_ROWS_LIVE) * N_KEYS)
            ht = jnp.dot(u_ref[rows, :], x1t_ref[...], preferred_element_type=F32)
            act = (0.5 * ht * (1.0 + lax.erf(ht * (1.0 / math.sqrt(2.0))))).astype(BF16)
            gh_ref[rows, :] = gh_ref[rows, :] * act
    for r0 in range(0, D, ACC_ROWS):
        acc_ref[r0:r0 + ACC_ROWS, :] += jnp.dot(vt_ref[r0:r0 + ACC_ROWS, :], gh_ref[...],
                                                preferred_element_type=F32)

    @pl.when(c == pl.num_programs(1) - 1)
    def _():
        y = ALPHA * x1_ref[...] + acc_ref[...].T
        o_ref[...] = _layer_norm(y, g_ref[...], b_ref[...])


def _peer_experts(x1t, x1, u_bf, vt_bf, route, ln_g, ln_b):
    t = x1.shape[0]
    tm, ec = 512, 1024
    ns = tm // LANES
    r2, e2, n1, e1 = route
    ni = ec // N_KEYS
    blk = pl.BlockSpec((ns, PEER_HEADS, N_KEYS, LANES), lambda i, c: (i, 0, 0, 0))
    row = pl.BlockSpec((ns, PEER_HEADS, ni, LANES), lambda i, c: (i, 0, c, 0))
    vec = pl.BlockSpec((1, D), lambda i, c: (0, 0))
    return pl.pallas_call(
        _experts_body,
        grid=(t // tm, N_EXPERTS // ec),
        in_specs=[
            pl.BlockSpec((D, tm), lambda i, c: (0, i)),
            pl.BlockSpec((tm, D), lambda i, c: (i, 0), pipeline_mode=pl.Buffered(1)),
            pl.BlockSpec((ec, D), lambda i, c: (c, 0)),
            pl.BlockSpec((D, ec), lambda i, c: (0, c)),
            blk, blk, row, row, vec, vec,
        ],
        out_specs=pl.BlockSpec((tm, D), lambda i, c: (i, 0)),
        out_shape=jax.ShapeDtypeStruct((t, D), F32),
        scratch_shapes=[pltpu.VMEM((D, tm), F32), pltpu.VMEM((ec, tm), BF16)],
        compiler_params=_params(("parallel", "arbitrary")),
        name="peer_experts",
    )(x1t, x1, u_bf, vt_bf, r2, e2, n1, e1, ln_g.reshape(1, -1), ln_b.reshape(1, -1))


def _rope_tables(seq):
    half = HEAD_DIM // 2
    inv_freq = ROPE_THETA ** (-jnp.arange(half, dtype=F32) / half)
    ang = jnp.arange(seq, dtype=F32)[:, None] * inv_freq[None, :]
    cos, sin = jnp.cos(ang), jnp.sin(ang)
    reps = LANES // HEAD_DIM
    cos_t = jnp.tile(jnp.concatenate([cos, cos], axis=1), (1, reps))
    sin_t = jnp.tile(jnp.concatenate([-sin, sin], axis=1), (1, reps))
    return cos_t, sin_t


def _encoder_layer(x, w, tables):
    batch, seq, _ = x.shape
    x2 = x.reshape(batch * seq, D)
    qkv = _qkv_proj(x2, w["w_in"], tables[0], tables[1], batch, seq)
    hc = _glu_proj(x2, w["w_in"])
    attn_parts = [_dilated_attention(qkv[g], g) for g in range(N_GROUPS)]
    hcn = _conv_module(hc, w["conv_w"], w["conv_b"], w["conv_ln_g"], w["conv_ln_b"], batch, seq)
    attn = _attn_merge(attn_parts, batch, seq)
    x1, x1t = _merge_layer(x2, attn, hcn, w["w_in"], w["w_attn_o"], w["w_conv_o"], w["w_out"],
                           w["ln1_g"], w["ln1_b"])
    route = _peer_route(x1t, w["wq_t"], w["keys"])
    y = _peer_experts(x1t, x1, w["u"], w["v_t"], route, w["ln2_g"], w["ln2_b"])
    return y.reshape(batch, seq, D)


def kernel(x_prompt, x_sample, w_in, w_attn_o, conv_w, conv_b, conv_ln_g, conv_ln_b, w_conv_o, w_out,
           ln1_g, ln1_b, peer_w_q, peer_sub_keys, peer_u, peer_v, ln2_g, ln2_b):
    assert w_in.shape[0] == DEPTH == 1
    w = {
        "w_in": w_in[0].astype(BF16),
        "w_attn_o": w_attn_o[0].astype(BF16),
        "w_conv_o": w_conv_o[0].astype(BF16),
        "w_out": w_out[0].astype(BF16),
        "conv_w": conv_w[0], "conv_b": conv_b[0], "conv_ln_g": conv_ln_g[0], "conv_ln_b": conv_ln_b[0],
        "ln1_g": ln1_g[0], "ln1_b": ln1_b[0], "ln2_g": ln2_g[0], "ln2_b": ln2_b[0],
        "wq_t": peer_w_q[0].T.astype(BF16),
        "keys": peer_sub_keys[0].astype(BF16),
        "u": peer_u[0].astype(BF16),
        "v_t": peer_v[0].T.astype(BF16),
    }
    tables = _rope_tables(x_prompt.shape[1])
    assert x_sample.shape[1] == x_prompt.shape[1]
    return (_encoder_layer(x_prompt, w, tables), _encoder_layer(x_sample, w, tables))
```

```python
import functools
import math

import jax
import jax.numpy as jnp
from jax import lax
from jax.experimental import pallas as pl
from jax.experimental.pallas import tpu as pltpu

D = 2048
HEAD_DIM = 64
HEADS = 8
GROUP_W = HEADS * HEAD_DIM
DILATED_GROUPS = ((128, 1), (512, 4), (2048, 16))
N_GROUPS = len(DILATED_GROUPS)
ATTN_W = N_GROUPS * GROUP_W
BLK = 64
ROPE_THETA = 10000.0
CONV_C = D // 2
CONV_K = 31
IN_COLS = 3 * ATTN_W + 2 * CONV_C + 2 * D
GLU_COL0 = 3 * ATTN_W
GATE_COL0 = GLU_COL0 + 2 * CONV_C
PEER_HEADS = 8
N_KEYS = 128
N_EXPERTS = N_KEYS * N_KEYS
TOPK = 16
LN_EPS = 1e-5
DEPTH = 1
ALPHA = (2.0 * DEPTH) ** 0.25

LANES = 128
SUBLANES = 8
BF16_ROWS = 2 * SUBLANES
GATE_ROWS_LIVE = 2
DOT_KEYS = 4
ACC_ROWS = 512
EXPERT_CHUNK = 1024
ATTN_ROWS = 128
VMEM_LIMIT = 56 * 1024 * 1024

F32 = jnp.float32
BF16 = jnp.bfloat16
NEG_INF = float("-inf")


def _params(semantics):
    return pltpu.CompilerParams(dimension_semantics=semantics, vmem_limit_bytes=VMEM_LIMIT)


def _layer_norm(y, g, b):
    mu = jnp.mean(y, axis=-1, keepdims=True)
    yc = y - mu
    var = jnp.mean(yc * yc, axis=-1, keepdims=True)
    return yc * lax.rsqrt(var + LN_EPS) * g + b


def _qkv_body(x_ref, w_ref, cos_ref, sin_ref, o0_ref, o1_ref, o2_ref, xb_ref, res_ref):
    @pl.when(pl.program_id(1) == 0)
    def _():
        xb_ref[...] = x_ref[...].astype(BF16)

    acc = jnp.dot(xb_ref[...], w_ref[...], preferred_element_type=F32)
    c = cos_ref[...]
    s = sin_ref[...]
    lane = lax.broadcasted_iota(jnp.int32, c.shape, 1)
    first_half = (lane % HEAD_DIM) < (HEAD_DIM // 2)
    for cb in range(ATTN_W // LANES):
        xc = acc[:, cb * LANES:(cb + 1) * LANES]
        partner = jnp.where(first_half,
                            pltpu.roll(xc, LANES - HEAD_DIM // 2, 1),
                            pltpu.roll(xc, HEAD_DIM // 2, 1))
        res_ref[cb] = xc * c + partner * s

    tm = res_ref.shape[1]
    per_group = GROUP_W // LANES
    for g, o_ref in enumerate((o0_ref, o1_ref, o2_ref)):
        d = DILATED_GROUPS[g][1]
        for r in range(d):
            for cb in range(per_group):
                rows = res_ref[g * per_group + cb, pl.ds(r, tm // d, stride=d), :]
                o_ref[r, :, cb * LANES:(cb + 1) * LANES] = rows.astype(o_ref.dtype)


def _qkv_proj(x2, w_in_bf, cos_t, sin_t, batch, seq):
    t = x2.shape[0]
    tm = 512
    nseq = seq // tm
    out_specs, out_shape = [], []
    for _, d in DILATED_GROUPS:
        assert tm % (d * BF16_ROWS) == 0
        out_specs.append(pl.BlockSpec((None, d, tm // d, GROUP_W), lambda i, j: (i // nseq, 0, i % nseq, j)))
        out_shape.append(jax.ShapeDtypeStruct((batch, d, seq // d, 3 * GROUP_W), BF16))
    return pl.pallas_call(
        _qkv_body,
        grid=(t // tm, 3),
        in_specs=[
            pl.BlockSpec((tm, D), lambda i, j: (i, 0)),
            pl.BlockSpec((D, ATTN_W), lambda i, j: (0, j)),
            pl.BlockSpec((None, tm, LANES), lambda i, j: (j, i % nseq, 0)),
            pl.BlockSpec((None, tm, LANES), lambda i, j: (j, i % nseq, 0)),
        ],
        out_specs=out_specs,
        out_shape=out_shape,
        scratch_shapes=[pltpu.VMEM((tm, D), BF16), pltpu.VMEM((ATTN_W // LANES, tm, LANES), F32)],
        compiler_params=_params(("parallel", "arbitrary")),
        name="qkv_proj",
    )(x2, w_in_bf, cos_t, sin_t)


def _glu_body(x_ref, wa_ref, wb_ref, o_ref, xb_ref):
    @pl.when(pl.program_id(1) == 0)
    def _():
        xb_ref[...] = x_ref[...].astype(BF16)

    xb = xb_ref[...]
    a = jnp.dot(xb, wa_ref[...], preferred_element_type=F32)
    b = jnp.dot(xb, wb_ref[...], preferred_element_type=F32)
    o_ref[...] = a * jax.nn.sigmoid(b)


def _glu_proj(x2, w_in_bf):
    t = x2.shape[0]
    tm, tn = 512, 512
    a0 = GLU_COL0 // tn
    b0 = (GLU_COL0 + CONV_C) // tn
    return pl.pallas_call(
        _glu_body,
        grid=(t // tm, CONV_C // tn),
        in_specs=[
            pl.BlockSpec((tm, D), lambda i, j: (i, 0)),
            pl.BlockSpec((D, tn), lambda i, j: (0, a0 + j)),
            pl.BlockSpec((D, tn), lambda i, j: (0, b0 + j)),
        ],
        out_specs=pl.BlockSpec((tm, tn), lambda i, j: (i, j)),
        out_shape=jax.ShapeDtypeStruct((t, CONV_C), F32),
        scratch_shapes=[pltpu.VMEM((tm, D), BF16)],
        compiler_params=_params(("parallel", "arbitrary")),
        name="glu_proj",
    )(x2, w_in_bf, w_in_bf)


def _attn_body(q_ref, kp_ref, km_ref, kn_ref, vp_ref, vm_ref, vn_ref, o_ref, lse_ref, *, sub_len):
    n = pl.program_id(2)
    tq = q_ref.shape[0]
    k = jnp.concatenate([kp_ref[...], km_ref[...], kn_ref[...]], axis=0)
    v = jnp.concatenate([vp_ref[...], vm_ref[...], vn_ref[...]], axis=0)
    tk = ATTN_ROWS + 2 * BLK
    row = lax.broadcasted_iota(jnp.int32, (ATTN_ROWS, tk), 0)
    col = lax.broadcasted_iota(jnp.int32, (ATTN_ROWS, tk), 1)
    band = (col >= row) & (col <= row + 2 * BLK)
    heads = [slice(h * HEAD_DIM, (h + 1) * HEAD_DIM) for h in range(HEADS)]
    for r0 in range(0, tq, ATTN_ROWS):
        pos = n * tq + r0 - BLK + col
        valid = band & (pos >= 0) & (pos < sub_len)
        scores = [lax.dot_general(q_ref[r0:r0 + ATTN_ROWS, sl], k[r0:r0 + tk, sl], (((1,), (1,)), ((), ())),
                                  preferred_element_type=F32) for sl in heads]
        probs, stats = [], []
        for s in scores:
            s = jnp.where(valid, s, NEG_INF)
            m = jnp.max(s, axis=-1, keepdims=True)
            p = jnp.exp(s - m)
            l = jnp.sum(p, axis=-1, keepdims=True)
            probs.append(p.astype(BF16))
            stats.append((m, l))
        outs = [jnp.dot(p, v[r0:r0 + tk, sl], preferred_element_type=F32) for p, sl in zip(probs, heads)]
        for sl, o, (m, l) in zip(heads, outs, stats):
            o_ref[r0:r0 + ATTN_ROWS, sl] = o / l
            lse_ref[r0:r0 + ATTN_ROWS, sl] = jnp.broadcast_to(m + jnp.log(l), (ATTN_ROWS, HEAD_DIM))


def _dilated_attention(qkv_g, g):
    window, d = DILATED_GROUPS[g]
    batch, _, sub_len, _ = qkv_g.shape
    assert window // (2 * d) == BLK and sub_len % BLK == 0
    tq = min(256, sub_len)
    nb = sub_len // BLK

    def main(col):
        return pl.BlockSpec((None, None, tq, GROUP_W), lambda b, r, n: (b, r, n, col))

    def prev(col):
        return pl.BlockSpec((None, None, BLK, GROUP_W),
                            lambda b, r, n: (b, r, jnp.maximum(n * (tq // BLK) - 1, 0), col))

    def nxt(col):
        return pl.BlockSpec((None, None, BLK, GROUP_W),
                            lambda b, r, n: (b, r, jnp.minimum((n + 1) * (tq // BLK), nb - 1), col))

    return pl.pallas_call(
        functools.partial(_attn_body, sub_len=sub_len),
        grid=(batch, d, sub_len // tq),
        in_specs=[main(0), prev(1), main(1), nxt(1), prev(2), main(2), nxt(2)],
        out_specs=[main(0), main(0)],
        out_shape=[jax.ShapeDtypeStruct((batch, d, sub_len, GROUP_W), F32)] * 2,
        compiler_params=_params(("parallel", "parallel", "arbitrary")),
        name=f"dilated_attn_{g}",
    )(qkv_g, qkv_g, qkv_g, qkv_g, qkv_g, qkv_g, qkv_g)


CONV_HALO = 16
CONV_ROWS = 64


def _conv_body(hp_ref, hm_ref, hn_ref, w_ref, b_ref, g_ref, beta_ref, o_ref, win_ref, sh_ref, y_ref):
    n = pl.program_id(1)
    tc = hm_ref.shape[0]
    win_ref[0:CONV_HALO, :] = jnp.where(n > 0, hp_ref[...], 0.0)
    win_ref[CONV_HALO:CONV_HALO + tc, :] = hm_ref[...]
    win_ref[CONV_HALO + tc:, :] = jnp.where(n < pl.num_programs(1) - 1, hn_ref[...], 0.0)
    off = CONV_HALO - CONV_K // 2
    span = tc + CONV_HALO + SUBLANES
    for b in range(1, SUBLANES):
        sh_ref[b - 1] = win_ref[b:b + span, :]
    for cb in range(CONV_C // LANES):
        cs = slice(cb * LANES, (cb + 1) * LANES)
        for r0 in range(0, tc, CONV_ROWS):
            acc = jnp.zeros((CONV_ROWS, LANES), F32)
            for j in range(CONV_K):
                a, b = divmod(off + j, SUBLANES)
                rows = slice(r0 + a * SUBLANES, r0 + a * SUBLANES + CONV_ROWS)
                tap = win_ref[rows, cs] if b == 0 else sh_ref[b - 1, rows, cs]
                acc = acc + tap * w_ref[j:j + 1, cs]
            y_ref[r0:r0 + CONV_ROWS, cs] = acc
    y = _layer_norm(y_ref[...] + b_ref[...], g_ref[...], beta_ref[...])
    o_ref[...] = (y * jax.nn.sigmoid(y)).astype(o_ref.dtype)


def _conv_module(hc, conv_w, conv_b, ln_g, ln_b, batch, seq):
    tc = 256
    nh = seq // CONV_HALO
    hc3 = hc.reshape(batch, seq, CONV_C)
    vec = pl.BlockSpec((1, CONV_C), lambda b, n: (0, 0))
    out = pl.pallas_call(
        _conv_body,
        grid=(batch, seq // tc),
        in_specs=[
            pl.BlockSpec((None, CONV_HALO, CONV_C),
                         lambda b, n: (b, jnp.maximum(n * (tc // CONV_HALO) - 1, 0), 0)),
            pl.BlockSpec((None, tc, CONV_C), lambda b, n: (b, n, 0)),
            pl.BlockSpec((None, CONV_HALO, CONV_C),
                         lambda b, n: (b, jnp.minimum((n + 1) * (tc // CONV_HALO), nh - 1), 0)),
            pl.BlockSpec((CONV_K, CONV_C), lambda b, n: (0, 0)),
            vec, vec, vec,
        ],
        out_specs=pl.BlockSpec((None, tc, CONV_C), lambda b, n: (b, n, 0)),
        out_shape=jax.ShapeDtypeStruct((batch, seq, CONV_C), BF16),
        scratch_shapes=[pltpu.VMEM((tc + 2 * CONV_HALO, CONV_C), F32),
                        pltpu.VMEM((SUBLANES - 1, tc + CONV_HALO + SUBLANES, CONV_C), F32),
                        pltpu.VMEM((tc, CONV_C), F32)],
        compiler_params=_params(("parallel", "arbitrary")),
        name="conv_module",
    )(hc3, hc3, hc3, conv_w, conv_b.reshape(1, -1), ln_g.reshape(1, -1), ln_b.reshape(1, -1))
    return out.reshape(batch * seq, CONV_C)


def _attn_merge_body(o0_ref, l0_ref, o1_ref, l1_ref, o2_ref, l2_ref, attn_ref, *scratch):
    tm = attn_ref.shape[0]
    for cb in range(GROUP_W // LANES):
        cs = slice(cb * LANES, (cb + 1) * LANES)
        parts = [(o0_ref[0, :, cs], l0_ref[0, :, cs])]
        for g, (o_ref, l_ref) in enumerate(((o1_ref, l1_ref), (o2_ref, l2_ref)), start=1):
            d = DILATED_GROUPS[g][1]
            o_s, l_s = scratch[2 * (g - 1)], scratch[2 * (g - 1) + 1]
            for r in range(d):
                o_s[pl.ds(r, tm // d, stride=d), :] = o_ref[r, :, cs]
                l_s[pl.ds(r, tm // d, stride=d), :] = l_ref[r, :, cs]
            parts.append((o_s[...], l_s[...]))
        (o0, l0), (o1, l1), (o2, l2) = parts
        m = jnp.maximum(jnp.maximum(l0, l1), l2)
        e0, e1, e2 = jnp.exp(l0 - m), jnp.exp(l1 - m), jnp.exp(l2 - m)
        attn = (e0 * o0 + e1 * o1 + e2 * o2) / (e0 + e1 + e2)
        attn_ref[:, cs] = attn.astype(attn_ref.dtype)


def _attn_merge(attn_parts, batch, seq):
    tm = 512
    in_specs, args = [], []
    for g, (o, l) in enumerate(attn_parts):
        d = DILATED_GROUPS[g][1]
        assert tm % (d * SUBLANES) == 0
        spec = pl.BlockSpec((None, d, tm // d, GROUP_W), lambda b, n: (b, 0, n, 0))
        in_specs += [spec, spec]
        args += [o, l]
    attn = pl.pallas_call(
        _attn_merge_body,
        grid=(batch, seq // tm),
        in_specs=in_specs,
        out_specs=pl.BlockSpec((None, tm, GROUP_W), lambda b, n: (b, n, 0)),
        out_shape=jax.ShapeDtypeStruct((batch, seq, GROUP_W), BF16),
        scratch_shapes=[pltpu.VMEM((tm, LANES), F32)] * 4,
        compiler_params=_params(("parallel", "parallel")),
        name="attn_merge",
    )(*args)
    return attn.reshape(batch * seq, GROUP_W)


def _merge_body(x_ref, attn_ref, hcn_ref, wga_ref, wgb_ref, wao_ref, wco_ref, wout_ref, g_ref, b_ref,
                x1_ref, x1t_ref, xb_ref, acc_ref):
    j = pl.program_id(1)

    @pl.when(j == 0)
    def _():
        xb_ref[...] = x_ref[...].astype(BF16)
        acc_ref[...] = jnp.zeros_like(acc_ref)

    xb = xb_ref[...]
    ga = jax.nn.sigmoid(jnp.dot(xb, wga_ref[...], preferred_element_type=F32))
    gb = jax.nn.sigmoid(jnp.dot(xb, wgb_ref[...], preferred_element_type=F32))
    br_a = jnp.dot(attn_ref[...], wao_ref[...], preferred_element_type=F32)
    br_b = jnp.dot(hcn_ref[...], wco_ref[...], preferred_element_type=F32)
    merged = (ga * br_a + gb * br_b).astype(BF16)
    acc_ref[...] += jnp.dot(merged, wout_ref[...], preferred_element_type=F32)

    @pl.when(j == pl.num_programs(1) - 1)
    def _():
        x1 = _layer_norm(ALPHA * x_ref[...] + acc_ref[...], g_ref[...], b_ref[...])
        x1_ref[...] = x1
        x1t_ref[...] = x1.T.astype(BF16)


def _merge_layer(x2, attn, hcn, w_in_bf, w_ao_bf, w_co_bf, w_out_bf, ln_g, ln_b):
    t = x2.shape[0]
    tm, tn = 512, 512
    nj = D // tn
    ga0 = GATE_COL0 // tn
    gb0 = (GATE_COL0 + D) // tn
    vec = pl.BlockSpec((1, D), lambda i, j: (0, 0))
    return pl.pallas_call(
        _merge_body,
        grid=(t // tm, nj),
        in_specs=[
            pl.BlockSpec((tm, D), lambda i, j: (i, 0)),
            pl.BlockSpec((tm, GROUP_W), lambda i, j: (i, 0)),
            pl.BlockSpec((tm, CONV_C), lambda i, j: (i, 0)),
            pl.BlockSpec((D, tn), lambda i, j: (0, ga0 + j)),
            pl.BlockSpec((D, tn), lambda i, j: (0, gb0 + j)),
            pl.BlockSpec((GROUP_W, tn), lambda i, j: (0, j)),
            pl.BlockSpec((CONV_C, tn), lambda i, j: (0, j)),
            pl.BlockSpec((tn, D), lambda i, j: (j, 0)),
            vec, vec,
        ],
        out_specs=[pl.BlockSpec((tm, D), lambda i, j: (i, 0)),
                   pl.BlockSpec((D, tm), lambda i, j: (0, i))],
        out_shape=[jax.ShapeDtypeStruct((t, D), F32), jax.ShapeDtypeStruct((D, t), BF16)],
        scratch_shapes=[pltpu.VMEM((tm, D), BF16), pltpu.VMEM((tm, D), F32)],
        compiler_params=_params(("parallel", "arbitrary")),
        name="merge_layer",
    )(x2, attn, hcn, w_in_bf, w_in_bf, w_ao_bf, w_co_bf, w_out_bf, ln_g.reshape(1, -1), ln_b.reshape(1, -1))


RANK_CODE_BASE = -(2.0 ** 120)


def _top16(s, exact_ties):
    row = lax.broadcasted_iota(jnp.int32, s.shape, 0).astype(F32)
    vals = []
    for k in range(TOPK):
        m = jnp.max(s, axis=0, keepdims=True)
        hit = s == m
        if exact_ties:
            hit = row == jnp.min(jnp.where(hit, row, float(N_KEYS)), axis=0, keepdims=True)
        s = jnp.where(hit, RANK_CODE_BASE * (1.0 + k / TOPK), s)
        vals.append(m)
    taken = s <= RANK_CODE_BASE
    rank = jnp.where(taken, (s * (1.0 / RANK_CODE_BASE) - 1.0) * TOPK, float(TOPK))
    n_taken = jnp.sum(jnp.where(taken, 1.0, 0.0), axis=0, keepdims=True)
    return rank, jnp.concatenate(vals, axis=0), n_taken


def _pair_counts(a1, a2, exact_ties):
    half = TOPK // 2
    top = a1[0:1, :] + a2[0:1, :]
    blocks = [a1 + a2[0:1, :]]
    rows_a = [lax.broadcasted_iota(jnp.int32, (TOPK, LANES), 0)]
    rows_b = [jnp.zeros((TOPK, LANES), jnp.int32)]
    for b in range(1, TOPK):
        blocks.append(a1[0:half, :] + a2[b:b + 1, :])
        rows_a.append(lax.broadcasted_iota(jnp.int32, (half, LANES), 0))
        rows_b.append(jnp.full((half, LANES), b, jnp.int32))
    ra = jnp.concatenate(rows_a, axis=0)
    rb = jnp.concatenate(rows_b, axis=0)
    cand0 = jnp.where((ra + 1) * (rb + 1) <= TOPK, jnp.concatenate(blocks, axis=0), NEG_INF)
    pos = (ra * TOPK + rb).astype(F32)
    cand = cand0
    for _ in range(TOPK):
        m = jnp.max(cand, axis=0, keepdims=True)
        hit = cand == m
        if exact_ties:
            hit = pos == jnp.min(jnp.where(hit, pos, float(TOPK * TOPK)), axis=0, keepdims=True)
        cand = jnp.where(hit, NEG_INF, cand)
    chosen = cand != cand0
    cnt = jnp.where(chosen, 1.0, 0.0)
    n_lo = cnt[0:half, :]
    for b in range(1, TOPK):
        n_lo = n_lo + cnt[TOPK + (b - 1) * half:TOPK + b * half, :]
    n = jnp.concatenate([n_lo, cnt[half:TOPK, :]], axis=0)
    z = jnp.sum(jnp.where(chosen, jnp.exp(cand0 - top), 0.0), axis=0, keepdims=True)
    return n, z, jnp.sum(n, axis=0, keepdims=True)


def _route_body(x1t_ref, wq_ref, keys_ref, r2_ref, e2_ref, n1_ref, e1_ref, q_ref, sc_ref):
    tm = x1t_ref.shape[1]
    ns = tm // LANES
    q_ref[...] = jnp.dot(wq_ref[...], x1t_ref[...], preferred_element_type=F32).astype(BF16)
    for hp in range(2 * PEER_HEADS):
        st = jnp.dot(keys_ref[hp % 2], q_ref[hp * N_KEYS:(hp + 1) * N_KEYS, :],
                     preferred_element_type=F32)
        for sl in range(ns):
            sc_ref[hp // 2, sl, hp % 2] = st[:, sl * LANES:(sl + 1) * LANES]

    def per_head_slab(idx, carry):
        h = idx // ns
        sl = idx % ns

        def route(exact_ties):
            s1 = sc_ref[h, sl, 0]
            s2 = sc_ref[h, sl, 1]
            r1, a1, t1 = _top16(s1, exact_ties)
            r2, a2, t2 = _top16(s2, exact_ties)
            n, z, tc = _pair_counts(a1, a2, exact_ties)
            n1 = jnp.zeros_like(r1)
            for a in range(TOPK):
                n1 = jnp.where(r1 == float(a), n[a:a + 1, :], n1)
            e1 = jnp.where(r1 < float(TOPK), jnp.exp(s1 - a1[0:1, :]), 0.0)
            e2 = jnp.where(r2 < float(TOPK), jnp.exp(s2 - a2[0:1, :]), 0.0) / z
            r2_ref[sl, h] = r2.astype(r2_ref.dtype)
            e2_ref[sl, h] = e2.astype(e2_ref.dtype)
            n1_ref[sl, h] = n1
            e1_ref[sl, h] = e1
            return (t1 == float(TOPK)) & (t2 == float(TOPK)) & (tc == float(TOPK))

        ok = route(False)

        @pl.when(jnp.max(jnp.where(ok, 0.0, 1.0)) > 0.0)
        def _():
            route(True)

        return carry

    lax.fori_loop(0, PEER_HEADS * ns, per_head_slab, 0)


def _peer_route(x1t, wq_t_bf, keys_bf):
    t = x1t.shape[1]
    tm = 512
    ns = tm // LANES
    blk = pl.BlockSpec((ns, PEER_HEADS, N_KEYS, LANES), lambda i: (i, 0, 0, 0))
    shape = (t // LANES, PEER_HEADS, N_KEYS, LANES)
    return pl.pallas_call(
        _route_body,
        grid=(t // tm,),
        in_specs=[
            pl.BlockSpec((D, tm), lambda i: (0, i)),
            pl.BlockSpec((D, D), lambda i: (0, 0)),
            pl.BlockSpec((2, N_KEYS, N_KEYS), lambda i: (0, 0, 0)),
        ],
        out_specs=[blk, blk, blk, blk],
        out_shape=[jax.ShapeDtypeStruct(shape, BF16), jax.ShapeDtypeStruct(shape, BF16),
                   jax.ShapeDtypeStruct(shape, F32), jax.ShapeDtypeStruct(shape, F32)],
        scratch_shapes=[pltpu.VMEM((D, tm), BF16), pltpu.VMEM((PEER_HEADS, ns, 2, N_KEYS, LANES), F32)],
        compiler_params=_params(("parallel",)),
        name="peer_route",
    )(x1t, wq_t_bf, keys_bf)


def _experts_body(x1t_ref, x1_ref, u_ref, vt_ref, r2_ref, e2_ref, n1_ref, e1_ref, g_ref, b_ref,
                  o_ref, acc_ref, gh_ref):
    c = pl.program_id(1)
    tm = x1t_ref.shape[1]
    ns = tm // LANES
    ni = u_ref.shape[0] // N_KEYS

    @pl.when(c == 0)
    def _():
        acc_ref[...] = jnp.zeros_like(acc_ref)

    nb = N_KEYS // BF16_ROWS
    zero = jnp.zeros((BF16_ROWS, LANES), BF16)
    for i0 in range(0, ni, GATE_ROWS_LIVE):
        for ts in range(ns):
            cs = slice(ts * LANES, (ts + 1) * LANES)
            gate = [[None] * nb for _ in range(GATE_ROWS_LIVE)]
            for h in range(PEER_HEADS):
                r2 = [r2_ref[ts, h, jb * BF16_ROWS:(jb + 1) * BF16_ROWS, :] for jb in range(nb)]
                e2 = [e2_ref[ts, h, jb * BF16_ROWS:(jb + 1) * BF16_ROWS, :] for jb in range(nb)]
                for ii in range(GATE_ROWS_LIVE):
                    i = i0 + ii
                    n1 = jnp.broadcast_to(n1_ref[ts, h, i:i + 1, :], (BF16_ROWS, LANES)).astype(BF16)
                    e1 = jnp.broadcast_to(e1_ref[ts, h, i:i + 1, :], (BF16_ROWS, LANES)).astype(BF16)
                    for jb in range(nb):
                        term = jnp.minimum(jnp.maximum(n1 - r2[jb], zero), e1) * e2[jb]
                        gate[ii][jb] = term if h == 0 else gate[ii][jb] + term
            for ii in range(GATE_ROWS_LIVE):
                for jb in range(nb):
                    r0 = (i0 + ii) * N_KEYS + jb * BF16_ROWS
                    gh_ref[r0:r0 + BF16_ROWS, cs] = gate[ii][jb]
        if (i0 + GATE_ROWS_LIVE) % DOT_KEYS == 0:
            rows = slice((i0 + GATE_ROWS_LIVE - DOT_KEYS) * N_KEYS, (i0 + GATE_ROWS_LIVE) * N_KEYS)
            ht = jnp.dot(u_ref[rows, :], x1t_ref[...], preferred_element_type=F32)
            act = (0.5 * ht * (1.0 + lax.erf(ht * (1.0 / math.sqrt(2.0))))).astype(BF16)
            gh_ref[rows, :] = gh_ref[rows, :] * act
    for r0 in range(0, D, ACC_ROWS):
        acc_ref[r0:r0 + ACC_ROWS, :] += jnp.dot(vt_ref[r0:r0 + ACC_ROWS, :], gh_ref[...],
                                                preferred_element_type=F32)

    @pl.when(c == pl.num_programs(1) - 1)
    def _():
        y = ALPHA * x1_ref[...] + acc_ref[...].T
        o_ref[...] = _layer_norm(y, g_ref[...], b_ref[...])


def _peer_experts(x1t, x1, u_bf, vt_bf, route, ln_g, ln_b):
    t = x1.shape[0]
    tm, ec = 512, EXPERT_CHUNK
    ns = tm // LANES
    r2, e2, n1, e1 = route
    ni = ec // N_KEYS
    blk = pl.BlockSpec((ns, PEER_HEADS, N_KEYS, LANES), lambda i, c: (i, 0, 0, 0))
    row = pl.BlockSpec((ns, PEER_HEADS, ni, LANES), lambda i, c: (i, 0, c, 0))
    vec = pl.BlockSpec((1, D), lambda i, c: (0, 0))
    return pl.pallas_call(
        _experts_body,
        grid=(t // tm, N_EXPERTS // ec),
        in_specs=[
            pl.BlockSpec((D, tm), lambda i, c: (0, i)),
            pl.BlockSpec((tm, D), lambda i, c: (i, 0), pipeline_mode=pl.Buffered(1)),
            pl.BlockSpec((ec, D), lambda i, c: (c, 0)),
            pl.BlockSpec((None, D, ec), lambda i, c: (c, 0, 0)),
            blk, blk, row, row, vec, vec,
        ],
        out_specs=pl.BlockSpec((tm, D), lambda i, c: (i, 0)),
        out_shape=jax.ShapeDtypeStruct((t, D), F32),
        scratch_shapes=[pltpu.VMEM((D, tm), F32), pltpu.VMEM((ec, tm), BF16)],
        compiler_params=_params(("parallel", "arbitrary")),
        name="peer_experts",
    )(x1t, x1, u_bf, vt_bf, r2, e2, n1, e1, ln_g.reshape(1, -1), ln_b.reshape(1, -1))


def _rope_tables(seq):
    half = HEAD_DIM // 2
    inv_freq = ROPE_THETA ** (-jnp.arange(half, dtype=F32) / half)
    ang = jnp.arange(seq, dtype=F32)[:, None] * inv_freq[None, :]
    cos, sin = jnp.cos(ang), jnp.sin(ang)
    reps = LANES // HEAD_DIM
    cos_t = jnp.tile(jnp.concatenate([cos, cos], axis=1), (1, reps))
    sin_t = jnp.tile(jnp.concatenate([-sin, sin], axis=1), (1, reps))
    scale = 1.0 / math.sqrt(HEAD_DIM)
    return (jnp.stack([cos_t * scale, cos_t, jnp.ones_like(cos_t)]),
            jnp.stack([sin_t * scale, sin_t, jnp.zeros_like(sin_t)]))


def _encoder_layer(x, w, tables):
    batch, seq, _ = x.shape
    x2 = x.reshape(batch * seq, D)
    qkv = _qkv_proj(x2, w["w_in"], tables[0], tables[1], batch, seq)
    hc = _glu_proj(x2, w["w_in"])
    attn_parts = [_dilated_attention(qkv[g], g) for g in range(N_GROUPS)]
    hcn = _conv_module(hc, w["conv_w"], w["conv_b"], w["conv_ln_g"], w["conv_ln_b"], batch, seq)
    attn = _attn_merge(attn_parts, batch, seq)
    x1, x1t = _merge_layer(x2, attn, hcn, w["w_in"], w["w_attn_o"], w["w_conv_o"], w["w_out"],
                           w["ln1_g"], w["ln1_b"])
    route = _peer_route(x1t, w["wq_t"], w["keys"])
    y = _peer_experts(x1t, x1, w["u"], w["v_t"], route, w["ln2_g"], w["ln2_b"])
    return y.reshape(batch, seq, D)


def kernel(x_prompt, x_sample, w_in, w_attn_o, conv_w, conv_b, conv_ln_g, conv_ln_b, w_conv_o, w_out,
           ln1_g, ln1_b, peer_w_q, peer_sub_keys, peer_u, peer_v, ln2_g, ln2_b):
    assert w_in.shape[0] == DEPTH == 1
    w = {
        "w_in": w_in[0].astype(BF16),
        "w_attn_o": w_attn_o[0].astype(BF16),
        "w_conv_o": w_conv_o[0].astype(BF16),
        "w_out": w_out[0].astype(BF16),
        "conv_w": conv_w[0], "conv_b": conv_b[0], "conv_ln_g": conv_ln_g[0], "conv_ln_b": conv_ln_b[0],
        "ln1_g": ln1_g[0], "ln1_b": ln1_b[0], "ln2_g": ln2_g[0], "ln2_b": ln2_b[0],
        "wq_t": peer_w_q[0].T.astype(BF16),
        "keys": peer_sub_keys[0].astype(BF16),
        "u": peer_u[0].astype(BF16),
        "v_t": peer_v[0].reshape(N_EXPERTS // EXPERT_CHUNK, EXPERT_CHUNK, D).transpose(0, 2, 1).astype(BF16),
    }
    tables = _rope_tables(x_prompt.shape[1])
    assert x_sample.shape[1] == x_prompt.shape[1]
    return (_encoder_layer(x_prompt, w, tables), _encoder_layer(x_sample, w, tables))
```

```python
import functools
import math

import jax
import jax.numpy as jnp
from jax import lax
from jax.experimental import pallas as pl
from jax.experimental.pallas import tpu as pltpu

D = 2048
HEAD_DIM = 64
HEADS = 8
GROUP_W = HEADS * HEAD_DIM
DILATED_GROUPS = ((128, 1), (512, 4), (2048, 16))
N_GROUPS = len(DILATED_GROUPS)
ATTN_W = N_GROUPS * GROUP_W
BLK = 64
ROPE_THETA = 10000.0
CONV_C = D // 2
CONV_K = 31
IN_COLS = 3 * ATTN_W + 2 * CONV_C + 2 * D
GLU_COL0 = 3 * ATTN_W
GATE_COL0 = GLU_COL0 + 2 * CONV_C
PEER_HEADS = 8
N_KEYS = 128
N_EXPERTS = N_KEYS * N_KEYS
TOPK = 16
LN_EPS = 1e-5
DEPTH = 1
ALPHA = (2.0 * DEPTH) ** 0.25

LANES = 128
SUBLANES = 8
BF16_ROWS = 2 * SUBLANES
GATE_ROWS_LIVE = 2
ACC_ROWS = 1024
TOKEN_COLS = 256
MXU_COLS = 256
EXPERT_CHUNK = 1024
ATTN_ROWS = 128
VMEM_LIMIT = 56 * 1024 * 1024

F32 = jnp.float32
BF16 = jnp.bfloat16
NEG_INF = float("-inf")


def _params(semantics):
    return pltpu.CompilerParams(dimension_semantics=semantics, vmem_limit_bytes=VMEM_LIMIT)


def _layer_norm(y, g, b):
    mu = jnp.mean(y, axis=-1, keepdims=True)
    yc = y - mu
    var = jnp.mean(yc * yc, axis=-1, keepdims=True)
    return yc * lax.rsqrt(var + LN_EPS) * g + b


def _qkv_body(x_ref, w_ref, cos_ref, sin_ref, o0_ref, o1_ref, o2_ref, xb_ref, res_ref):
    @pl.when(pl.program_id(1) == 0)
    def _():
        xb_ref[...] = x_ref[...].astype(BF16)

    acc = jnp.dot(xb_ref[...], w_ref[...], preferred_element_type=F32)
    c = cos_ref[...]
    s = sin_ref[...]
    lane = lax.broadcasted_iota(jnp.int32, c.shape, 1)
    first_half = (lane % HEAD_DIM) < (HEAD_DIM // 2)
    for cb in range(ATTN_W // LANES):
        xc = acc[:, cb * LANES:(cb + 1) * LANES]
        partner = jnp.where(first_half,
                            pltpu.roll(xc, LANES - HEAD_DIM // 2, 1),
                            pltpu.roll(xc, HEAD_DIM // 2, 1))
        res_ref[cb] = xc * c + partner * s

    tm = res_ref.shape[1]
    per_group = GROUP_W // LANES
    for g, o_ref in enumerate((o0_ref, o1_ref, o2_ref)):
        d = DILATED_GROUPS[g][1]
        for r in range(d):
            for cb in range(per_group):
                rows = res_ref[g * per_group + cb, pl.ds(r, tm // d, stride=d), :]
                o_ref[r, :, cb * LANES:(cb + 1) * LANES] = rows.astype(o_ref.dtype)


def _qkv_proj(x2, w_in_bf, cos_t, sin_t, batch, seq):
    t = x2.shape[0]
    tm = 512
    nseq = seq // tm
    out_specs, out_shape = [], []
    for _, d in DILATED_GROUPS:
        assert tm % (d * BF16_ROWS) == 0
        out_specs.append(pl.BlockSpec((None, d, tm // d, GROUP_W), lambda i, j: (i // nseq, 0, i % nseq, j)))
        out_shape.append(jax.ShapeDtypeStruct((batch, d, seq // d, 3 * GROUP_W), BF16))
    return pl.pallas_call(
        _qkv_body,
        grid=(t // tm, 3),
        in_specs=[
            pl.BlockSpec((tm, D), lambda i, j: (i, 0)),
            pl.BlockSpec((D, ATTN_W), lambda i, j: (0, j)),
            pl.BlockSpec((None, tm, LANES), lambda i, j: (j, i % nseq, 0)),
            pl.BlockSpec((None, tm, LANES), lambda i, j: (j, i % nseq, 0)),
        ],
        out_specs=out_specs,
        out_shape=out_shape,
        scratch_shapes=[pltpu.VMEM((tm, D), BF16), pltpu.VMEM((ATTN_W // LANES, tm, LANES), F32)],
        compiler_params=_params(("parallel", "arbitrary")),
        name="qkv_proj",
    )(x2, w_in_bf, cos_t, sin_t)


def _glu_body(x_ref, wa_ref, wb_ref, o_ref, xb_ref):
    @pl.when(pl.program_id(1) == 0)
    def _():
        xb_ref[...] = x_ref[...].astype(BF16)

    xb = xb_ref[...]
    for c0 in range(0, o_ref.shape[1], MXU_COLS):
        cs = slice(c0, c0 + MXU_COLS)
        a = jnp.dot(xb, wa_ref[:, cs], preferred_element_type=F32)
        b = jnp.dot(xb, wb_ref[:, cs], preferred_element_type=F32)
        o_ref[:, cs] = a * jax.nn.sigmoid(b)


def _glu_proj(x2, w_in_bf):
    t = x2.shape[0]
    tm, tn = 1024, 512
    a0 = GLU_COL0 // tn
    b0 = (GLU_COL0 + CONV_C) // tn
    return pl.pallas_call(
        _glu_body,
        grid=(t // tm, CONV_C // tn),
        in_specs=[
            pl.BlockSpec((tm, D), lambda i, j: (i, 0)),
            pl.BlockSpec((D, tn), lambda i, j: (0, a0 + j)),
            pl.BlockSpec((D, tn), lambda i, j: (0, b0 + j)),
        ],
        out_specs=pl.BlockSpec((tm, tn), lambda i, j: (i, j)),
        out_shape=jax.ShapeDtypeStruct((t, CONV_C), F32),
        scratch_shapes=[pltpu.VMEM((tm, D), BF16)],
        compiler_params=_params(("parallel", "arbitrary")),
        name="glu_proj",
    )(x2, w_in_bf, w_in_bf)


def _attn_body(q_ref, kp_ref, km_ref, kn_ref, vp_ref, vm_ref, vn_ref, o_ref, lse_ref, *, sub_len):
    n = pl.program_id(2)
    tq = q_ref.shape[0]
    k = jnp.concatenate([kp_ref[...], km_ref[...], kn_ref[...]], axis=0)
    v = jnp.concatenate([vp_ref[...], vm_ref[...], vn_ref[...]], axis=0)
    tk = ATTN_ROWS + 2 * BLK
    row = lax.broadcasted_iota(jnp.int32, (ATTN_ROWS, tk), 0)
    col = lax.broadcasted_iota(jnp.int32, (ATTN_ROWS, tk), 1)
    band = (col >= row) & (col <= row + 2 * BLK)
    heads = [slice(h * HEAD_DIM, (h + 1) * HEAD_DIM) for h in range(HEADS)]
    for r0 in range(0, tq, ATTN_ROWS):
        pos = n * tq + r0 - BLK + col
        valid = band & (pos >= 0) & (pos < sub_len)
        scores = [lax.dot_general(q_ref[r0:r0 + ATTN_ROWS, sl], k[r0:r0 + tk, sl], (((1,), (1,)), ((), ())),
                                  preferred_element_type=F32) for sl in heads]
        probs, stats = [], []
        for s in scores:
            s = jnp.where(valid, s, NEG_INF)
            m = jnp.max(s, axis=-1, keepdims=True)
            p = jnp.exp(s - m)
            l = jnp.sum(p, axis=-1, keepdims=True)
            probs.append(p.astype(BF16))
            stats.append((m, l))
        outs = [jnp.dot(p, v[r0:r0 + tk, sl], preferred_element_type=F32) for p, sl in zip(probs, heads)]
        for sl, o, (m, l) in zip(heads, outs, stats):
            o_ref[r0:r0 + ATTN_ROWS, sl] = o / l
            lse_ref[r0:r0 + ATTN_ROWS, sl] = jnp.broadcast_to(m + jnp.log(l), (ATTN_ROWS, HEAD_DIM))


def _dilated_attention(qkv_g, g):
    window, d = DILATED_GROUPS[g]
    batch, _, sub_len, _ = qkv_g.shape
    assert window // (2 * d) == BLK and sub_len % BLK == 0
    tq = min(256, sub_len)
    nb = sub_len // BLK

    def main(col):
        return pl.BlockSpec((None, None, tq, GROUP_W), lambda b, r, n: (b, r, n, col))

    def prev(col):
        return pl.BlockSpec((None, None, BLK, GROUP_W),
                            lambda b, r, n: (b, r, jnp.maximum(n * (tq // BLK) - 1, 0), col))

    def nxt(col):
        return pl.BlockSpec((None, None, BLK, GROUP_W),
                            lambda b, r, n: (b, r, jnp.minimum((n + 1) * (tq // BLK), nb - 1), col))

    return pl.pallas_call(
        functools.partial(_attn_body, sub_len=sub_len),
        grid=(batch, d, sub_len // tq),
        in_specs=[main(0), prev(1), main(1), nxt(1), prev(2), main(2), nxt(2)],
        out_specs=[main(0), main(0)],
        out_shape=[jax.ShapeDtypeStruct((batch, d, sub_len, GROUP_W), F32)] * 2,
        compiler_params=_params(("parallel", "parallel", "arbitrary")),
        name=f"dilated_attn_{g}",
    )(qkv_g, qkv_g, qkv_g, qkv_g, qkv_g, qkv_g, qkv_g)


CONV_HALO = 16
CONV_ROWS = 64


def _conv_body(hp_ref, hm_ref, hn_ref, w_ref, b_ref, g_ref, beta_ref, o_ref, win_ref, sh_ref, y_ref):
    n = pl.program_id(1)
    tc = hm_ref.shape[0]
    win_ref[0:CONV_HALO, :] = jnp.where(n > 0, hp_ref[...], 0.0)
    win_ref[CONV_HALO:CONV_HALO + tc, :] = hm_ref[...]
    win_ref[CONV_HALO + tc:, :] = jnp.where(n < pl.num_programs(1) - 1, hn_ref[...], 0.0)
    off = CONV_HALO - CONV_K // 2
    span = tc + CONV_HALO + SUBLANES
    for b in range(1, SUBLANES):
        sh_ref[b - 1] = win_ref[b:b + span, :]
    for cb in range(CONV_C // LANES):
        cs = slice(cb * LANES, (cb + 1) * LANES)
        for r0 in range(0, tc, CONV_ROWS):
            acc = jnp.zeros((CONV_ROWS, LANES), F32)
            for j in range(CONV_K):
                a, b = divmod(off + j, SUBLANES)
                rows = slice(r0 + a * SUBLANES, r0 + a * SUBLANES + CONV_ROWS)
                tap = win_ref[rows, cs] if b == 0 else sh_ref[b - 1, rows, cs]
                acc = acc + tap * w_ref[j:j + 1, cs]
            y_ref[r0:r0 + CONV_ROWS, cs] = acc
    y = _layer_norm(y_ref[...] + b_ref[...], g_ref[...], beta_ref[...])
    o_ref[...] = (y * jax.nn.sigmoid(y)).astype(o_ref.dtype)


def _conv_module(hc, conv_w, conv_b, ln_g, ln_b, batch, seq):
    tc = 256
    nh = seq // CONV_HALO
    hc3 = hc.reshape(batch, seq, CONV_C)
    vec = pl.BlockSpec((1, CONV_C), lambda b, n: (0, 0))
    out = pl.pallas_call(
        _conv_body,
        grid=(batch, seq // tc),
        in_specs=[
            pl.BlockSpec((None, CONV_HALO, CONV_C),
                         lambda b, n: (b, jnp.maximum(n * (tc // CONV_HALO) - 1, 0), 0)),
            pl.BlockSpec((None, tc, CONV_C), lambda b, n: (b, n, 0)),
            pl.BlockSpec((None, CONV_HALO, CONV_C),
                         lambda b, n: (b, jnp.minimum((n + 1) * (tc // CONV_HALO), nh - 1), 0)),
            pl.BlockSpec((CONV_K, CONV_C), lambda b, n: (0, 0)),
            vec, vec, vec,
        ],
        out_specs=pl.BlockSpec((None, tc, CONV_C), lambda b, n: (b, n, 0)),
        out_shape=jax.ShapeDtypeStruct((batch, seq, CONV_C), BF16),
        scratch_shapes=[pltpu.VMEM((tc + 2 * CONV_HALO, CONV_C), F32),
                        pltpu.VMEM((SUBLANES - 1, tc + CONV_HALO + SUBLANES, CONV_C), F32),
                        pltpu.VMEM((tc, CONV_C), F32)],
        compiler_params=_params(("parallel", "arbitrary")),
        name="conv_module",
    )(hc3, hc3, hc3, conv_w, conv_b.reshape(1, -1), ln_g.reshape(1, -1), ln_b.reshape(1, -1))
    return out.reshape(batch * seq, CONV_C)


def _attn_merge_body(o0_ref, l0_ref, o1_ref, l1_ref, o2_ref, l2_ref, attn_ref, *scratch):
    tm = attn_ref.shape[0]
    for cb in range(GROUP_W // LANES):
        cs = slice(cb * LANES, (cb + 1) * LANES)
        parts = [(o0_ref[0, :, cs], l0_ref[0, :, cs])]
        for g, (o_ref, l_ref) in enumerate(((o1_ref, l1_ref), (o2_ref, l2_ref)), start=1):
            d = DILATED_GROUPS[g][1]
            o_s, l_s = scratch[2 * (g - 1)], scratch[2 * (g - 1) + 1]
            for r in range(d):
                o_s[pl.ds(r, tm // d, stride=d), :] = o_ref[r, :, cs]
                l_s[pl.ds(r, tm // d, stride=d), :] = l_ref[r, :, cs]
            parts.append((o_s[...], l_s[...]))
        (o0, l0), (o1, l1), (o2, l2) = parts
        m = jnp.maximum(jnp.maximum(l0, l1), l2)
        e0, e1, e2 = jnp.exp(l0 - m), jnp.exp(l1 - m), jnp.exp(l2 - m)
        attn = (e0 * o0 + e1 * o1 + e2 * o2) / (e0 + e1 + e2)
        attn_ref[:, cs] = attn.astype(attn_ref.dtype)


def _attn_merge(attn_parts, batch, seq):
    tm = 512
    in_specs, args = [], []
    for g, (o, l) in enumerate(attn_parts):
        d = DILATED_GROUPS[g][1]
        assert tm % (d * SUBLANES) == 0
        spec = pl.BlockSpec((None, d, tm // d, GROUP_W), lambda b, n: (b, 0, n, 0))
        in_specs += [spec, spec]
        args += [o, l]
    attn = pl.pallas_call(
        _attn_merge_body,
        grid=(batch, seq // tm),
        in_specs=in_specs,
        out_specs=pl.BlockSpec((None, tm, GROUP_W), lambda b, n: (b, n, 0)),
        out_shape=jax.ShapeDtypeStruct((batch, seq, GROUP_W), BF16),
        scratch_shapes=[pltpu.VMEM((tm, LANES), F32)] * 4,
        compiler_params=_params(("parallel", "parallel")),
        name="attn_merge",
    )(*args)
    return attn.reshape(batch * seq, GROUP_W)


def _merge_body(x_ref, attn_ref, hcn_ref, wga_ref, wgb_ref, wao_ref, wco_ref, wout_ref, g_ref, b_ref,
                x1_ref, x1t_ref, xb_ref, acc_ref):
    j = pl.program_id(1)

    @pl.when(j == 0)
    def _():
        xb_ref[...] = x_ref[...].astype(BF16)
        acc_ref[...] = jnp.zeros_like(acc_ref)

    xb = xb_ref[...]
    ga = jax.nn.sigmoid(jnp.dot(xb, wga_ref[...], preferred_element_type=F32))
    gb = jax.nn.sigmoid(jnp.dot(xb, wgb_ref[...], preferred_element_type=F32))
    br_a = jnp.dot(attn_ref[...], wao_ref[...], preferred_element_type=F32)
    br_b = jnp.dot(hcn_ref[...], wco_ref[...], preferred_element_type=F32)
    merged = (ga * br_a + gb * br_b).astype(BF16)
    acc_ref[...] += jnp.dot(merged, wout_ref[...], preferred_element_type=F32)

    @pl.when(j == pl.num_programs(1) - 1)
    def _():
        x1 = _layer_norm(ALPHA * x_ref[...] + acc_ref[...], g_ref[...], b_ref[...])
        x1_ref[...] = x1
        x1t_ref[...] = x1.T.astype(BF16)


def _merge_layer(x2, attn, hcn, w_in_bf, w_ao_bf, w_co_bf, w_out_bf, ln_g, ln_b):
    t = x2.shape[0]
    tm, tn = 512, 512
    nj = D // tn
    ga0 = GATE_COL0 // tn
    gb0 = (GATE_COL0 + D) // tn
    vec = pl.BlockSpec((1, D), lambda i, j: (0, 0))
    return pl.pallas_call(
        _merge_body,
        grid=(t // tm, nj),
        in_specs=[
            pl.BlockSpec((tm, D), lambda i, j: (i, 0)),
            pl.BlockSpec((tm, GROUP_W), lambda i, j: (i, 0)),
            pl.BlockSpec((tm, CONV_C), lambda i, j: (i, 0)),
            pl.BlockSpec((D, tn), lambda i, j: (0, ga0 + j)),
            pl.BlockSpec((D, tn), lambda i, j: (0, gb0 + j)),
            pl.BlockSpec((GROUP_W, tn), lambda i, j: (0, j)),
            pl.BlockSpec((CONV_C, tn), lambda i, j: (0, j)),
            pl.BlockSpec((tn, D), lambda i, j: (j, 0)),
            vec, vec,
        ],
        out_specs=[pl.BlockSpec((tm, D), lambda i, j: (i, 0)),
                   pl.BlockSpec((D, tm), lambda i, j: (0, i))],
        out_shape=[jax.ShapeDtypeStruct((t, D), F32), jax.ShapeDtypeStruct((D, t), BF16)],
        scratch_shapes=[pltpu.VMEM((tm, D), BF16), pltpu.VMEM((tm, D), F32)],
        compiler_params=_params(("parallel", "arbitrary")),
        name="merge_layer",
    )(x2, attn, hcn, w_in_bf, w_in_bf, w_ao_bf, w_co_bf, w_out_bf, ln_g.reshape(1, -1), ln_b.reshape(1, -1))


RANK_CODE_BASE = -(2.0 ** 120)


def _top16(s, exact_ties):
    row = lax.broadcasted_iota(jnp.int32, s.shape, 0).astype(F32)
    vals = []
    for k in range(TOPK):
        m = jnp.max(s, axis=0, keepdims=True)
        hit = s == m
        if exact_ties:
            hit = row == jnp.min(jnp.where(hit, row, float(N_KEYS)), axis=0, keepdims=True)
        s = jnp.where(hit, RANK_CODE_BASE * (1.0 + k / TOPK), s)
        vals.append(m)
    taken = s <= RANK_CODE_BASE
    rank = jnp.where(taken, (s * (1.0 / RANK_CODE_BASE) - 1.0) * TOPK, float(TOPK))
    n_taken = jnp.sum(jnp.where(taken, 1.0, 0.0), axis=0, keepdims=True)
    return rank, jnp.concatenate(vals, axis=0), n_taken


def _pair_counts(a1, a2, exact_ties):
    half = TOPK // 2
    top = a1[0:1, :] + a2[0:1, :]
    blocks = [a1 + a2[0:1, :]]
    rows_a = [lax.broadcasted_iota(jnp.int32, (TOPK, LANES), 0)]
    rows_b = [jnp.zeros((TOPK, LANES), jnp.int32)]
    for b in range(1, TOPK):
        blocks.append(a1[0:half, :] + a2[b:b + 1, :])
        rows_a.append(lax.broadcasted_iota(jnp.int32, (half, LANES), 0))
        rows_b.append(jnp.full((half, LANES), b, jnp.int32))
    ra = jnp.concatenate(rows_a, axis=0)
    rb = jnp.concatenate(rows_b, axis=0)
    cand0 = jnp.where((ra + 1) * (rb + 1) <= TOPK, jnp.concatenate(blocks, axis=0), NEG_INF)
    pos = (ra * TOPK + rb).astype(F32)
    cand = cand0
    for _ in range(TOPK):
        m = jnp.max(cand, axis=0, keepdims=True)
        hit = cand == m
        if exact_ties:
            hit = pos == jnp.min(jnp.where(hit, pos, float(TOPK * TOPK)), axis=0, keepdims=True)
        cand = jnp.where(hit, NEG_INF, cand)
    chosen = cand != cand0
    cnt = jnp.where(chosen, 1.0, 0.0)
    n_lo = cnt[0:half, :]
    for b in range(1, TOPK):
        n_lo = n_lo + cnt[TOPK + (b - 1) * half:TOPK + b * half, :]
    n = jnp.concatenate([n_lo, cnt[half:TOPK, :]], axis=0)
    z = jnp.sum(jnp.where(chosen, jnp.exp(cand0 - top), 0.0), axis=0, keepdims=True)
    return n, z, jnp.sum(n, axis=0, keepdims=True)


def _route_body(x1t_ref, wq_ref, keys_ref, r2_ref, e2_ref, n1_ref, e1_ref, q_ref, sc_ref):
    tm = x1t_ref.shape[1]
    ns = tm // LANES
    q_ref[...] = jnp.dot(wq_ref[...], x1t_ref[...], preferred_element_type=F32).astype(BF16)
    for hp in range(2 * PEER_HEADS):
        st = jnp.dot(keys_ref[hp % 2], q_ref[hp * N_KEYS:(hp + 1) * N_KEYS, :],
                     preferred_element_type=F32)
        for sl in range(ns):
            sc_ref[hp // 2, sl, hp % 2] = st[:, sl * LANES:(sl + 1) * LANES]

    def per_head_slab(idx, carry):
        h = idx // ns
        sl = idx % ns

        def route(exact_ties):
            s1 = sc_ref[h, sl, 0]
            s2 = sc_ref[h, sl, 1]
            r1, a1, t1 = _top16(s1, exact_ties)
            r2, a2, t2 = _top16(s2, exact_ties)
            n, z, tc = _pair_counts(a1, a2, exact_ties)
            n1 = jnp.zeros_like(r1)
            for a in range(TOPK):
                n1 = jnp.where(r1 == float(a), n[a:a + 1, :], n1)
            e1 = jnp.where(r1 < float(TOPK), jnp.exp(s1 - a1[0:1, :]), 0.0)
            e2 = jnp.where(r2 < float(TOPK), jnp.exp(s2 - a2[0:1, :]), 0.0) / z
            r2_ref[sl, h] = r2.astype(r2_ref.dtype)
            e2_ref[sl, h] = e2.astype(e2_ref.dtype)
            n1_ref[sl, h] = n1
            e1_ref[sl, h] = e1
            return (t1 == float(TOPK)) & (t2 == float(TOPK)) & (tc == float(TOPK))

        ok = route(False)

        @pl.when(jnp.max(jnp.where(ok, 0.0, 1.0)) > 0.0)
        def _():
            route(True)

        return carry

    lax.fori_loop(0, PEER_HEADS * ns, per_head_slab, 0)


def _peer_route(x1t, wq_t_bf, keys_bf):
    t = x1t.shape[1]
    tm = 512
    ns = tm // LANES
    blk = pl.BlockSpec((ns, PEER_HEADS, N_KEYS, LANES), lambda i: (i, 0, 0, 0))
    shape = (t // LANES, PEER_HEADS, N_KEYS, LANES)
    return pl.pallas_call(
        _route_body,
        grid=(t // tm,),
        in_specs=[
            pl.BlockSpec((D, tm), lambda i: (0, i)),
            pl.BlockSpec((D, D), lambda i: (0, 0)),
            pl.BlockSpec((2, N_KEYS, N_KEYS), lambda i: (0, 0, 0)),
        ],
        out_specs=[blk, blk, blk, blk],
        out_shape=[jax.ShapeDtypeStruct(shape, BF16), jax.ShapeDtypeStruct(shape, BF16),
                   jax.ShapeDtypeStruct(shape, F32), jax.ShapeDtypeStruct(shape, F32)],
        scratch_shapes=[pltpu.VMEM((D, tm), BF16), pltpu.VMEM((PEER_HEADS, ns, 2, N_KEYS, LANES), F32)],
        compiler_params=_params(("parallel",)),
        name="peer_route",
    )(x1t, wq_t_bf, keys_bf)


def _experts_body(x1t_ref, x1_ref, u_ref, vt_ref, r2_ref, e2_ref, n1_ref, e1_ref, g_ref, b_ref,
                  o_ref, acc_ref, gh_ref):
    c = pl.program_id(1)
    tm = x1t_ref.shape[1]
    ns = tm // LANES
    ni = u_ref.shape[0] // N_KEYS

    @pl.when(c == 0)
    def _():
        acc_ref[...] = jnp.zeros_like(acc_ref)

    nb = N_KEYS // BF16_ROWS
    zero = jnp.zeros((BF16_ROWS, LANES), BF16)
    for i0 in range(0, ni, GATE_ROWS_LIVE):
        for ts in range(ns):
            cs = slice(ts * LANES, (ts + 1) * LANES)
            gate = [[None] * nb for _ in range(GATE_ROWS_LIVE)]
            for h in range(PEER_HEADS):
                r2 = [r2_ref[ts, h, jb * BF16_ROWS:(jb + 1) * BF16_ROWS, :] for jb in range(nb)]
                e2 = [e2_ref[ts, h, jb * BF16_ROWS:(jb + 1) * BF16_ROWS, :] for jb in range(nb)]
                for ii in range(GATE_ROWS_LIVE):
                    i = i0 + ii
                    n1 = jnp.broadcast_to(n1_ref[ts, h, i:i + 1, :], (BF16_ROWS, LANES)).astype(BF16)
                    e1 = jnp.broadcast_to(e1_ref[ts, h, i:i + 1, :], (BF16_ROWS, LANES)).astype(BF16)
                    for jb in range(nb):
                        term = jnp.minimum(jnp.maximum(n1 - r2[jb], zero), e1) * e2[jb]
                        gate[ii][jb] = term if h == 0 else gate[ii][jb] + term
            for ii in range(GATE_ROWS_LIVE):
                for jb in range(nb):
                    r0 = (i0 + ii) * N_KEYS + jb * BF16_ROWS
                    gh_ref[r0:r0 + BF16_ROWS, cs] = gate[ii][jb]
    for t0 in range(0, tm, TOKEN_COLS):
        cs = slice(t0, t0 + TOKEN_COLS)
        ht = jnp.dot(u_ref[...], x1t_ref[:, cs], preferred_element_type=F32)
        act = (0.5 * ht * (1.0 + lax.erf(ht * (1.0 / math.sqrt(2.0))))).astype(BF16)
        gh_ref[:, cs] = gh_ref[:, cs] * act
    for t0 in range(0, tm, TOKEN_COLS):
        cs = slice(t0, t0 + TOKEN_COLS)
        for r0 in range(0, D, ACC_ROWS):
            acc_ref[r0:r0 + ACC_ROWS, cs] += jnp.dot(vt_ref[r0:r0 + ACC_ROWS, :], gh_ref[:, cs],
                                                     preferred_element_type=F32)

    @pl.when(c == pl.num_programs(1) - 1)
    def _():
        y = ALPHA * x1_ref[...] + acc_ref[...].T
        o_ref[...] = _layer_norm(y, g_ref[...], b_ref[...])


def _peer_experts(x1t, x1, u_bf, vt_bf, route, ln_g, ln_b):
    t = x1.shape[0]
    tm, ec = 512, EXPERT_CHUNK
    ns = tm // LANES
    r2, e2, n1, e1 = route
    ni = ec // N_KEYS
    blk = pl.BlockSpec((ns, PEER_HEADS, N_KEYS, LANES), lambda i, c: (i, 0, 0, 0))
    row = pl.BlockSpec((ns, PEER_HEADS, ni, LANES), lambda i, c: (i, 0, c, 0))
    vec = pl.BlockSpec((1, D), lambda i, c: (0, 0))
    return pl.pallas_call(
        _experts_body,
        grid=(t // tm, N_EXPERTS // ec),
        in_specs=[
            pl.BlockSpec((D, tm), lambda i, c: (0, i)),
            pl.BlockSpec((tm, D), lambda i, c: (i, 0), pipeline_mode=pl.Buffered(1)),
            pl.BlockSpec((ec, D), lambda i, c: (c, 0)),
            pl.BlockSpec((None, D, ec), lambda i, c: (c, 0, 0)),
            blk, blk, row, row, vec, vec,
        ],
        out_specs=pl.BlockSpec((tm, D), lambda i, c: (i, 0)),
        out_shape=jax.ShapeDtypeStruct((t, D), F32),
        scratch_shapes=[pltpu.VMEM((D, tm), F32), pltpu.VMEM((ec, tm), BF16)],
        compiler_params=_params(("parallel", "arbitrary")),
        name="peer_experts",
    )(x1t, x1, u_bf, vt_bf, r2, e2, n1, e1, ln_g.reshape(1, -1), ln_b.reshape(1, -1))


def _rope_tables(seq):
    half = HEAD_DIM // 2
    inv_freq = ROPE_THETA ** (-jnp.arange(half, dtype=F32) / half)
    ang = jnp.arange(seq, dtype=F32)[:, None] * inv_freq[None, :]
    cos, sin = jnp.cos(ang), jnp.sin(ang)
    reps = LANES // HEAD_DIM
    cos_t = jnp.tile(jnp.concatenate([cos, cos], axis=1), (1, reps))
    sin_t = jnp.tile(jnp.concatenate([-sin, sin], axis=1), (1, reps))
    scale = 1.0 / math.sqrt(HEAD_DIM)
    return (jnp.stack([cos_t * scale, cos_t, jnp.ones_like(cos_t)]),
            jnp.stack([sin_t * scale, sin_t, jnp.zeros_like(sin_t)]))


def _encoder_layer(x, w, tables):
    batch, seq, _ = x.shape
    x2 = x.reshape(batch * seq, D)
    qkv = _qkv_proj(x2, w["w_in"], tables[0], tables[1], batch, seq)
    hc = _glu_proj(x2, w["w_in"])
    attn_parts = [_dilated_attention(qkv[g], g) for g in range(N_GROUPS)]
    hcn = _conv_module(hc, w["conv_w"], w["conv_b"], w["conv_ln_g"], w["conv_ln_b"], batch, seq)
    attn = _attn_merge(attn_parts, batch, seq)
    x1, x1t = _merge_layer(x2, attn, hcn, w["w_in"], w["w_attn_o"], w["w_conv_o"], w["w_out"],
                           w["ln1_g"], w["ln1_b"])
    route = _peer_route(x1t, w["wq_t"], w["keys"])
    y = _peer_experts(x1t, x1, w["u"], w["v_t"], route, w["ln2_g"], w["ln2_b"])
    return y.reshape(batch, seq, D)


def kernel(x_prompt, x_sample, w_in, w_attn_o, conv_w, conv_b, conv_ln_g, conv_ln_b, w_conv_o, w_out,
           ln1_g, ln1_b, peer_w_q, peer_sub_keys, peer_u, peer_v, ln2_g, ln2_b):
    assert w_in.shape[0] == DEPTH == 1
    w = {
        "w_in": w_in[0].astype(BF16),
        "w_attn_o": w_attn_o[0].astype(BF16),
        "w_conv_o": w_conv_o[0].astype(BF16),
        "w_out": w_out[0].astype(BF16),
        "conv_w": conv_w[0], "conv_b": conv_b[0], "conv_ln_g": conv_ln_g[0], "conv_ln_b": conv_ln_b[0],
        "ln1_g": ln1_g[0], "ln1_b": ln1_b[0], "ln2_g": ln2_g[0], "ln2_b": ln2_b[0],
        "wq_t": peer_w_q[0].T.astype(BF16),
        "keys": peer_sub_keys[0].astype(BF16),
        "u": peer_u[0].astype(BF16),
        "v_t": peer_v[0].reshape(N_EXPERTS // EXPERT_CHUNK, EXPERT_CHUNK, D).transpose(0, 2, 1).astype(BF16),
    }
    tables = _rope_tables(x_prompt.shape[1])
    assert x_sample.shape[1] == x_prompt.shape[1]
    return (_encoder_layer(x_prompt, w, tables), _encoder_layer(x_sample, w, tables))
```

```python
import functools
import math

import jax
import jax.numpy as jnp
from jax import lax
from jax.experimental import pallas as pl
from jax.experimental.pallas import tpu as pltpu

D = 2048
HEAD_DIM = 64
HEADS = 8
GROUP_W = HEADS * HEAD_DIM
DILATED_GROUPS = ((128, 1), (512, 4), (2048, 16))
N_GROUPS = len(DILATED_GROUPS)
ATTN_W = N_GROUPS * GROUP_W
BLK = 64
ROPE_THETA = 10000.0
CONV_C = D // 2
CONV_K = 31
IN_COLS = 3 * ATTN_W + 2 * CONV_C + 2 * D
GLU_COL0 = 3 * ATTN_W
GATE_COL0 = GLU_COL0 + 2 * CONV_C
PEER_HEADS = 8
N_KEYS = 128
N_EXPERTS = N_KEYS * N_KEYS
TOPK = 16
LN_EPS = 1e-5
DEPTH = 1
ALPHA = (2.0 * DEPTH) ** 0.25

LANES = 128
SUBLANES = 8
BF16_ROWS = 2 * SUBLANES
GATE_ROWS_LIVE = 2
ACC_ROWS = 1024
TOKEN_COLS = 256
MXU_COLS = 256
EXPERT_CHUNK = 512
ATTN_ROWS = 128
VMEM_LIMIT = 56 * 1024 * 1024

F32 = jnp.float32
BF16 = jnp.bfloat16
NEG_INF = float("-inf")


def _params(semantics):
    return pltpu.CompilerParams(dimension_semantics=semantics, vmem_limit_bytes=VMEM_LIMIT)


def _layer_norm(y, g, b):
    mu = jnp.mean(y, axis=-1, keepdims=True)
    yc = y - mu
    var = jnp.mean(yc * yc, axis=-1, keepdims=True)
    return yc * lax.rsqrt(var + LN_EPS) * g + b


def _qkv_body(x_ref, w_ref, cos_ref, sin_ref, o0_ref, o1_ref, o2_ref, xb_ref, res_ref):
    @pl.when(pl.program_id(1) == 0)
    def _():
        xb_ref[...] = x_ref[...].astype(BF16)

    acc = jnp.dot(xb_ref[...], w_ref[...], preferred_element_type=F32)
    c = cos_ref[...]
    s = sin_ref[...]
    lane = lax.broadcasted_iota(jnp.int32, c.shape, 1)
    first_half = (lane % HEAD_DIM) < (HEAD_DIM // 2)
    for cb in range(ATTN_W // LANES):
        xc = acc[:, cb * LANES:(cb + 1) * LANES]
        partner = jnp.where(first_half,
                            pltpu.roll(xc, LANES - HEAD_DIM // 2, 1),
                            pltpu.roll(xc, HEAD_DIM // 2, 1))
        res_ref[cb] = xc * c + partner * s

    tm = res_ref.shape[1]
    per_group = GROUP_W // LANES
    for g, o_ref in enumerate((o0_ref, o1_ref, o2_ref)):
        d = DILATED_GROUPS[g][1]
        for r in range(d):
            for cb in range(per_group):
                rows = res_ref[g * per_group + cb, pl.ds(r, tm // d, stride=d), :]
                o_ref[r, :, cb * LANES:(cb + 1) * LANES] = rows.astype(o_ref.dtype)


def _qkv_proj(x2, w_in_bf, cos_t, sin_t, batch, seq):
    t = x2.shape[0]
    tm = 1024
    nseq = seq // tm
    out_specs, out_shape = [], []
    for _, d in DILATED_GROUPS:
        assert tm % (d * BF16_ROWS) == 0
        out_specs.append(pl.BlockSpec((None, d, tm // d, GROUP_W), lambda i, j: (i // nseq, 0, i % nseq, j)))
        out_shape.append(jax.ShapeDtypeStruct((batch, d, seq // d, 3 * GROUP_W), BF16))
    return pl.pallas_call(
        _qkv_body,
        grid=(t // tm, 3),
        in_specs=[
            pl.BlockSpec((tm, D), lambda i, j: (i, 0)),
            pl.BlockSpec((D, ATTN_W), lambda i, j: (0, j)),
            pl.BlockSpec((None, tm, LANES), lambda i, j: (j, i % nseq, 0)),
            pl.BlockSpec((None, tm, LANES), lambda i, j: (j, i % nseq, 0)),
        ],
        out_specs=out_specs,
        out_shape=out_shape,
        scratch_shapes=[pltpu.VMEM((tm, D), BF16), pltpu.VMEM((ATTN_W // LANES, tm, LANES), F32)],
        compiler_params=_params(("parallel", "arbitrary")),
        name="qkv_proj",
    )(x2, w_in_bf, cos_t, sin_t)


def _glu_body(x_ref, wa_ref, wb_ref, o_ref, xb_ref):
    @pl.when(pl.program_id(1) == 0)
    def _():
        xb_ref[...] = x_ref[...].astype(BF16)

    xb = xb_ref[...]
    for c0 in range(0, o_ref.shape[1], MXU_COLS):
        cs = slice(c0, c0 + MXU_COLS)
        a = jnp.dot(xb, wa_ref[:, cs], preferred_element_type=F32)
        b = jnp.dot(xb, wb_ref[:, cs], preferred_element_type=F32)
        o_ref[:, cs] = a * jax.nn.sigmoid(b)


def _glu_proj(x2, w_in_bf):
    t = x2.shape[0]
    tm, tn = 1024, 512
    a0 = GLU_COL0 // tn
    b0 = (GLU_COL0 + CONV_C) // tn
    return pl.pallas_call(
        _glu_body,
        grid=(t // tm, CONV_C // tn),
        in_specs=[
            pl.BlockSpec((tm, D), lambda i, j: (i, 0)),
            pl.BlockSpec((D, tn), lambda i, j: (0, a0 + j)),
            pl.BlockSpec((D, tn), lambda i, j: (0, b0 + j)),
        ],
        out_specs=pl.BlockSpec((tm, tn), lambda i, j: (i, j)),
        out_shape=jax.ShapeDtypeStruct((t, CONV_C), F32),
        scratch_shapes=[pltpu.VMEM((tm, D), BF16)],
        compiler_params=_params(("parallel", "arbitrary")),
        name="glu_proj",
    )(x2, w_in_bf, w_in_bf)


def _attn_body(q_ref, kp_ref, km_ref, kn_ref, vp_ref, vm_ref, vn_ref, o_ref, lse_ref, *, sub_len):
    n = pl.program_id(2)
    tq = q_ref.shape[0]
    k = jnp.concatenate([kp_ref[...], km_ref[...], kn_ref[...]], axis=0)
    v = jnp.concatenate([vp_ref[...], vm_ref[...], vn_ref[...]], axis=0)
    tk = ATTN_ROWS + 2 * BLK
    row = lax.broadcasted_iota(jnp.int32, (ATTN_ROWS, tk), 0)
    col = lax.broadcasted_iota(jnp.int32, (ATTN_ROWS, tk), 1)
    band = (col >= row) & (col <= row + 2 * BLK)
    heads = [slice(h * HEAD_DIM, (h + 1) * HEAD_DIM) for h in range(HEADS)]
    for r0 in range(0, tq, ATTN_ROWS):
        pos = n * tq + r0 - BLK + col
        valid = band & (pos >= 0) & (pos < sub_len)
        scores = [lax.dot_general(q_ref[r0:r0 + ATTN_ROWS, sl], k[r0:r0 + tk, sl], (((1,), (1,)), ((), ())),
                                  preferred_element_type=F32) for sl in heads]
        probs, stats = [], []
        for s in scores:
            s = jnp.where(valid, s, NEG_INF)
            m = jnp.max(s, axis=-1, keepdims=True)
            p = jnp.exp(s - m)
            l = jnp.sum(p, axis=-1, keepdims=True)
            probs.append(p.astype(BF16))
            stats.append((m, l))
        outs = [jnp.dot(p, v[r0:r0 + tk, sl], preferred_element_type=F32) for p, sl in zip(probs, heads)]
        for sl, o, (m, l) in zip(heads, outs, stats):
            o_ref[r0:r0 + ATTN_ROWS, sl] = o / l
            lse_ref[r0:r0 + ATTN_ROWS, sl] = jnp.broadcast_to(m + jnp.log(l), (ATTN_ROWS, HEAD_DIM))


def _dilated_attention(qkv_g, g):
    window, d = DILATED_GROUPS[g]
    batch, _, sub_len, _ = qkv_g.shape
    assert window // (2 * d) == BLK and sub_len % BLK == 0
    tq = min(256, sub_len)
    nb = sub_len // BLK

    def main(col):
        return pl.BlockSpec((None, None, tq, GROUP_W), lambda b, r, n: (b, r, n, col))

    def prev(col):
        return pl.BlockSpec((None, None, BLK, GROUP_W),
                            lambda b, r, n: (b, r, jnp.maximum(n * (tq // BLK) - 1, 0), col))

    def nxt(col):
        return pl.BlockSpec((None, None, BLK, GROUP_W),
                            lambda b, r, n: (b, r, jnp.minimum((n + 1) * (tq // BLK), nb - 1), col))

    return pl.pallas_call(
        functools.partial(_attn_body, sub_len=sub_len),
        grid=(batch, d, sub_len // tq),
        in_specs=[main(0), prev(1), main(1), nxt(1), prev(2), main(2), nxt(2)],
        out_specs=[main(0), main(0)],
        out_shape=[jax.ShapeDtypeStruct((batch, d, sub_len, GROUP_W), F32)] * 2,
        compiler_params=_params(("parallel", "parallel", "arbitrary")),
        name=f"dilated_attn_{g}",
    )(qkv_g, qkv_g, qkv_g, qkv_g, qkv_g, qkv_g, qkv_g)


CONV_HALO = 16
CONV_ROWS = 64


def _conv_body(hp_ref, hm_ref, hn_ref, w_ref, b_ref, g_ref, beta_ref, o_ref, win_ref, sh_ref, y_ref):
    n = pl.program_id(1)
    tc = hm_ref.shape[0]
    win_ref[0:CONV_HALO, :] = jnp.where(n > 0, hp_ref[...], 0.0)
    win_ref[CONV_HALO:CONV_HALO + tc, :] = hm_ref[...]
    win_ref[CONV_HALO + tc:, :] = jnp.where(n < pl.num_programs(1) - 1, hn_ref[...], 0.0)
    off = CONV_HALO - CONV_K // 2
    span = tc + CONV_HALO + SUBLANES
    for b in range(1, SUBLANES):
        sh_ref[b - 1] = win_ref[b:b + span, :]
    for cb in range(CONV_C // LANES):
        cs = slice(cb * LANES, (cb + 1) * LANES)
        for r0 in range(0, tc, CONV_ROWS):
            acc = jnp.zeros((CONV_ROWS, LANES), F32)
            for j in range(CONV_K):
                a, b = divmod(off + j, SUBLANES)
                rows = slice(r0 + a * SUBLANES, r0 + a * SUBLANES + CONV_ROWS)
                tap = win_ref[rows, cs] if b == 0 else sh_ref[b - 1, rows, cs]
                acc = acc + tap * w_ref[j:j + 1, cs]
            y_ref[r0:r0 + CONV_ROWS, cs] = acc
    y = _layer_norm(y_ref[...] + b_ref[...], g_ref[...], beta_ref[...])
    o_ref[...] = (y * jax.nn.sigmoid(y)).astype(o_ref.dtype)


def _conv_module(hc, conv_w, conv_b, ln_g, ln_b, batch, seq):
    tc = 256
    nh = seq // CONV_HALO
    hc3 = hc.reshape(batch, seq, CONV_C)
    vec = pl.BlockSpec((1, CONV_C), lambda b, n: (0, 0))
    out = pl.pallas_call(
        _conv_body,
        grid=(batch, seq // tc),
        in_specs=[
            pl.BlockSpec((None, CONV_HALO, CONV_C),
                         lambda b, n: (b, jnp.maximum(n * (tc // CONV_HALO) - 1, 0), 0)),
            pl.BlockSpec((None, tc, CONV_C), lambda b, n: (b, n, 0)),
            pl.BlockSpec((None, CONV_HALO, CONV_C),
                         lambda b, n: (b, jnp.minimum((n + 1) * (tc // CONV_HALO), nh - 1), 0)),
            pl.BlockSpec((CONV_K, CONV_C), lambda b, n: (0, 0)),
            vec, vec, vec,
        ],
        out_specs=pl.BlockSpec((None, tc, CONV_C), lambda b, n: (b, n, 0)),
        out_shape=jax.ShapeDtypeStruct((batch, seq, CONV_C), BF16),
        scratch_shapes=[pltpu.VMEM((tc + 2 * CONV_HALO, CONV_C), F32),
                        pltpu.VMEM((SUBLANES - 1, tc + CONV_HALO + SUBLANES, CONV_C), F32),
                        pltpu.VMEM((tc, CONV_C), F32)],
        compiler_params=_params(("parallel", "arbitrary")),
        name="conv_module",
    )(hc3, hc3, hc3, conv_w, conv_b.reshape(1, -1), ln_g.reshape(1, -1), ln_b.reshape(1, -1))
    return out.reshape(batch * seq, CONV_C)


def _attn_merge_body(o0_ref, l0_ref, o1_ref, l1_ref, o2_ref, l2_ref, attn_ref, *scratch):
    tm = attn_ref.shape[0]
    for cb in range(GROUP_W // LANES):
        cs = slice(cb * LANES, (cb + 1) * LANES)
        parts = [(o0_ref[0, :, cs], l0_ref[0, :, cs])]
        for g, (o_ref, l_ref) in enumerate(((o1_ref, l1_ref), (o2_ref, l2_ref)), start=1):
            d = DILATED_GROUPS[g][1]
            o_s, l_s = scratch[2 * (g - 1)], scratch[2 * (g - 1) + 1]
            for r in range(d):
                o_s[pl.ds(r, tm // d, stride=d), :] = o_ref[r, :, cs]
                l_s[pl.ds(r, tm // d, stride=d), :] = l_ref[r, :, cs]
            parts.append((o_s[...], l_s[...]))
        (o0, l0), (o1, l1), (o2, l2) = parts
        m = jnp.maximum(jnp.maximum(l0, l1), l2)
        e0, e1, e2 = jnp.exp(l0 - m), jnp.exp(l1 - m), jnp.exp(l2 - m)
        attn = (e0 * o0 + e1 * o1 + e2 * o2) / (e0 + e1 + e2)
        attn_ref[:, cs] = attn.astype(attn_ref.dtype)


def _attn_merge(attn_parts, batch, seq):
    tm = 512
    in_specs, args = [], []
    for g, (o, l) in enumerate(attn_parts):
        d = DILATED_GROUPS[g][1]
        assert tm % (d * SUBLANES) == 0
        spec = pl.BlockSpec((None, d, tm // d, GROUP_W), lambda b, n: (b, 0, n, 0))
        in_specs += [spec, spec]
        args += [o, l]
    attn = pl.pallas_call(
        _attn_merge_body,
        grid=(batch, seq // tm),
        in_specs=in_specs,
        out_specs=pl.BlockSpec((None, tm, GROUP_W), lambda b, n: (b, n, 0)),
        out_shape=jax.ShapeDtypeStruct((batch, seq, GROUP_W), BF16),
        scratch_shapes=[pltpu.VMEM((tm, LANES), F32)] * 4,
        compiler_params=_params(("parallel", "parallel")),
        name="attn_merge",
    )(*args)
    return attn.reshape(batch * seq, GROUP_W)


def _merge_body(x_ref, attn_ref, hcn_ref, wga_ref, wgb_ref, wao_ref, wco_ref, wout_ref, g_ref, b_ref,
                x1_ref, x1t_ref, xb_ref, acc_ref):
    j = pl.program_id(1)

    @pl.when(j == 0)
    def _():
        xb_ref[...] = x_ref[...].astype(BF16)
        acc_ref[...] = jnp.zeros_like(acc_ref)

    xb = xb_ref[...]
    ga = jax.nn.sigmoid(jnp.dot(xb, wga_ref[...], preferred_element_type=F32))
    gb = jax.nn.sigmoid(jnp.dot(xb, wgb_ref[...], preferred_element_type=F32))
    br_a = jnp.dot(attn_ref[...], wao_ref[...], preferred_element_type=F32)
    br_b = jnp.dot(hcn_ref[...], wco_ref[...], preferred_element_type=F32)
    merged = (ga * br_a + gb * br_b).astype(BF16)
    acc_ref[...] += jnp.dot(merged, wout_ref[...], preferred_element_type=F32)

    @pl.when(j == pl.num_programs(1) - 1)
    def _():
        x1 = _layer_norm(ALPHA * x_ref[...] + acc_ref[...], g_ref[...], b_ref[...])
        x1_ref[...] = x1
        x1t_ref[...] = x1.T.astype(BF16)


def _merge_layer(x2, attn, hcn, w_in_bf, w_ao_bf, w_co_bf, w_out_bf, ln_g, ln_b):
    t = x2.shape[0]
    tm, tn = 512, 512
    nj = D // tn
    ga0 = GATE_COL0 // tn
    gb0 = (GATE_COL0 + D) // tn
    vec = pl.BlockSpec((1, D), lambda i, j: (0, 0))
    return pl.pallas_call(
        _merge_body,
        grid=(t // tm, nj),
        in_specs=[
            pl.BlockSpec((tm, D), lambda i, j: (i, 0)),
            pl.BlockSpec((tm, GROUP_W), lambda i, j: (i, 0)),
            pl.BlockSpec((tm, CONV_C), lambda i, j: (i, 0)),
            pl.BlockSpec((D, tn), lambda i, j: (0, ga0 + j)),
            pl.BlockSpec((D, tn), lambda i, j: (0, gb0 + j)),
            pl.BlockSpec((GROUP_W, tn), lambda i, j: (0, j)),
            pl.BlockSpec((CONV_C, tn), lambda i, j: (0, j)),
            pl.BlockSpec((tn, D), lambda i, j: (j, 0)),
            vec, vec,
        ],
        out_specs=[pl.BlockSpec((tm, D), lambda i, j: (i, 0)),
                   pl.BlockSpec((D, tm), lambda i, j: (0, i))],
        out_shape=[jax.ShapeDtypeStruct((t, D), F32), jax.ShapeDtypeStruct((D, t), BF16)],
        scratch_shapes=[pltpu.VMEM((tm, D), BF16), pltpu.VMEM((tm, D), F32)],
        compiler_params=_params(("parallel", "arbitrary")),
        name="merge_layer",
    )(x2, attn, hcn, w_in_bf, w_in_bf, w_ao_bf, w_co_bf, w_out_bf, ln_g.reshape(1, -1), ln_b.reshape(1, -1))


RANK_CODE_BASE = -(2.0 ** 120)


def _top16(s, exact_ties):
    row = lax.broadcasted_iota(jnp.int32, s.shape, 0).astype(F32)
    vals = []
    for k in range(TOPK):
        m = jnp.max(s, axis=0, keepdims=True)
        hit = s == m
        if exact_ties:
            hit = row == jnp.min(jnp.where(hit, row, float(N_KEYS)), axis=0, keepdims=True)
        s = jnp.where(hit, RANK_CODE_BASE * (1.0 + k / TOPK), s)
        vals.append(m)
    taken = s <= RANK_CODE_BASE
    rank = jnp.where(taken, (s * (1.0 / RANK_CODE_BASE) - 1.0) * TOPK, float(TOPK))
    n_taken = jnp.sum(jnp.where(taken, 1.0, 0.0), axis=0, keepdims=True)
    return rank, jnp.concatenate(vals, axis=0), n_taken


def _pair_counts(a1, a2, exact_ties):
    half = TOPK // 2
    top = a1[0:1, :] + a2[0:1, :]
    blocks = [a1 + a2[0:1, :]]
    rows_a = [lax.broadcasted_iota(jnp.int32, (TOPK, LANES), 0)]
    rows_b = [jnp.zeros((TOPK, LANES), jnp.int32)]
    for b in range(1, TOPK):
        blocks.append(a1[0:half, :] + a2[b:b + 1, :])
        rows_a.append(lax.broadcasted_iota(jnp.int32, (half, LANES), 0))
        rows_b.append(jnp.full((half, LANES), b, jnp.int32))
    ra = jnp.concatenate(rows_a, axis=0)
    rb = jnp.concatenate(rows_b, axis=0)
    cand0 = jnp.where((ra + 1) * (rb + 1) <= TOPK, jnp.concatenate(blocks, axis=0), NEG_INF)
    pos = (ra * TOPK + rb).astype(F32)
    cand = cand0
    for _ in range(TOPK):
        m = jnp.max(cand, axis=0, keepdims=True)
        hit = cand == m
        if exact_ties:
            hit = pos == jnp.min(jnp.where(hit, pos, float(TOPK * TOPK)), axis=0, keepdims=True)
        cand = jnp.where(hit, NEG_INF, cand)
    chosen = cand != cand0
    cnt = jnp.where(chosen, 1.0, 0.0)
    n_lo = cnt[0:half, :]
    for b in range(1, TOPK):
        n_lo = n_lo + cnt[TOPK + (b - 1) * half:TOPK + b * half, :]
    n = jnp.concatenate([n_lo, cnt[half:TOPK, :]], axis=0)
    z = jnp.sum(jnp.where(chosen, jnp.exp(cand0 - top), 0.0), axis=0, keepdims=True)
    return n, z, jnp.sum(n, axis=0, keepdims=True)


def _route_body(x1t_ref, wq_ref, keys_ref, r2_ref, e2_ref, n1_ref, e1_ref, q_ref, sc_ref):
    tm = x1t_ref.shape[1]
    ns = tm // LANES
    q_ref[...] = jnp.dot(wq_ref[...], x1t_ref[...], preferred_element_type=F32).astype(BF16)
    for hp in range(2 * PEER_HEADS):
        st = jnp.dot(keys_ref[hp % 2], q_ref[hp * N_KEYS:(hp + 1) * N_KEYS, :],
                     preferred_element_type=F32)
        for sl in range(ns):
            sc_ref[hp // 2, sl, hp % 2] = st[:, sl * LANES:(sl + 1) * LANES]

    def per_head_slab(idx, carry):
        h = idx // ns
        sl = idx % ns

        def route(exact_ties):
            s1 = sc_ref[h, sl, 0]
            s2 = sc_ref[h, sl, 1]
            r1, a1, t1 = _top16(s1, exact_ties)
            r2, a2, t2 = _top16(s2, exact_ties)
            n, z, tc = _pair_counts(a1, a2, exact_ties)
            n1 = jnp.zeros_like(r1)
            for a in range(TOPK):
                n1 = jnp.where(r1 == float(a), n[a:a + 1, :], n1)
            e1 = jnp.where(r1 < float(TOPK), jnp.exp(s1 - a1[0:1, :]), 0.0)
            e2 = jnp.where(r2 < float(TOPK), jnp.exp(s2 - a2[0:1, :]), 0.0) / z
            r2_ref[sl, h] = r2.astype(r2_ref.dtype)
            e2_ref[sl, h] = e2.astype(e2_ref.dtype)
            n1_ref[sl, h] = n1
            e1_ref[sl, h] = e1
            return (t1 == float(TOPK)) & (t2 == float(TOPK)) & (tc == float(TOPK))

        ok = route(False)

        @pl.when(jnp.max(jnp.where(ok, 0.0, 1.0)) > 0.0)
        def _():
            route(True)

        return carry

    lax.fori_loop(0, PEER_HEADS * ns, per_head_slab, 0)


def _peer_route(x1t, wq_t_bf, keys_bf):
    t = x1t.shape[1]
    tm = 512
    ns = tm // LANES
    blk = pl.BlockSpec((ns, PEER_HEADS, N_KEYS, LANES), lambda i: (i, 0, 0, 0))
    shape = (t // LANES, PEER_HEADS, N_KEYS, LANES)
    return pl.pallas_call(
        _route_body,
        grid=(t // tm,),
        in_specs=[
            pl.BlockSpec((D, tm), lambda i: (0, i)),
            pl.BlockSpec((D, D), lambda i: (0, 0)),
            pl.BlockSpec((2, N_KEYS, N_KEYS), lambda i: (0, 0, 0)),
        ],
        out_specs=[blk, blk, blk, blk],
        out_shape=[jax.ShapeDtypeStruct(shape, BF16), jax.ShapeDtypeStruct(shape, BF16),
                   jax.ShapeDtypeStruct(shape, F32), jax.ShapeDtypeStruct(shape, F32)],
        scratch_shapes=[pltpu.VMEM((D, tm), BF16), pltpu.VMEM((PEER_HEADS, ns, 2, N_KEYS, LANES), F32)],
        compiler_params=_params(("parallel",)),
        name="peer_route",
    )(x1t, wq_t_bf, keys_bf)


def _experts_body(x1t_ref, x1_ref, u_ref, vt_ref, r2_ref, e2_ref, n1_ref, e1_ref, g_ref, b_ref,
                  o_ref, acc_ref, gh_ref):
    c = pl.program_id(1)
    tm = x1t_ref.shape[1]
    ns = tm // LANES
    ni = u_ref.shape[0] // N_KEYS

    @pl.when(c == 0)
    def _():
        acc_ref[...] = jnp.zeros_like(acc_ref)

    nb = N_KEYS // BF16_ROWS
    zero = jnp.zeros((BF16_ROWS, LANES), BF16)
    for i0 in range(0, ni, GATE_ROWS_LIVE):
        for ts in range(ns):
            cs = slice(ts * LANES, (ts + 1) * LANES)
            gate = [[None] * nb for _ in range(GATE_ROWS_LIVE)]
            for h in range(PEER_HEADS):
                r2 = [r2_ref[ts, h, jb * BF16_ROWS:(jb + 1) * BF16_ROWS, :] for jb in range(nb)]
                e2 = [e2_ref[ts, h, jb * BF16_ROWS:(jb + 1) * BF16_ROWS, :] for jb in range(nb)]
                for ii in range(GATE_ROWS_LIVE):
                    i = (c % (SUBLANES // ni)) * ni + i0 + ii
                    n1 = jnp.broadcast_to(n1_ref[ts, h, pl.ds(i, 1), :], (BF16_ROWS, LANES)).astype(BF16)
                    e1 = jnp.broadcast_to(e1_ref[ts, h, pl.ds(i, 1), :], (BF16_ROWS, LANES)).astype(BF16)
                    for jb in range(nb):
                        term = jnp.minimum(jnp.maximum(n1 - r2[jb], zero), e1) * e2[jb]
                        gate[ii][jb] = term if h == 0 else gate[ii][jb] + term
            for ii in range(GATE_ROWS_LIVE):
                for jb in range(nb):
                    r0 = (i0 + ii) * N_KEYS + jb * BF16_ROWS
                    gh_ref[r0:r0 + BF16_ROWS, cs] = gate[ii][jb]
    for t0 in range(0, tm, TOKEN_COLS):
        cs = slice(t0, t0 + TOKEN_COLS)
        ht = jnp.dot(u_ref[...], x1t_ref[:, cs], preferred_element_type=F32)
        act = (0.5 * ht * (1.0 + lax.erf(ht * (1.0 / math.sqrt(2.0))))).astype(BF16)
        gh_ref[:, cs] = gh_ref[:, cs] * act
    for t0 in range(0, tm, TOKEN_COLS):
        cs = slice(t0, t0 + TOKEN_COLS)
        for r0 in range(0, D, ACC_ROWS):
            acc_ref[r0:r0 + ACC_ROWS, cs] += jnp.dot(vt_ref[r0:r0 + ACC_ROWS, :], gh_ref[:, cs],
                                                     preferred_element_type=F32)

    @pl.when(c == pl.num_programs(1) - 1)
    def _():
        y = ALPHA * x1_ref[...] + acc_ref[...].T
        o_ref[...] = _layer_norm(y, g_ref[...], b_ref[...])


def _peer_experts(x1t, x1, u_bf, vt_bf, route, ln_g, ln_b):
    t = x1.shape[0]
    tm, ec = 1024, EXPERT_CHUNK
    ns = tm // LANES
    r2, e2, n1, e1 = route
    ni = ec // N_KEYS
    assert SUBLANES % ni == 0
    once = pl.Buffered(1)
    blk = pl.BlockSpec((ns, PEER_HEADS, N_KEYS, LANES), lambda i, c: (i, 0, 0, 0), pipeline_mode=once)
    row = pl.BlockSpec((ns, PEER_HEADS, SUBLANES, LANES), lambda i, c: (i, 0, c // (SUBLANES // ni), 0))
    vec = pl.BlockSpec((1, D), lambda i, c: (0, 0))
    return pl.pallas_call(
        _experts_body,
        grid=(t // tm, N_EXPERTS // ec),
        in_specs=[
            pl.BlockSpec((D, tm), lambda i, c: (0, i)),
            pl.BlockSpec((tm, D), lambda i, c: (i, 0), pipeline_mode=once),
            pl.BlockSpec((ec, D), lambda i, c: (c, 0)),
            pl.BlockSpec((None, D, ec), lambda i, c: (c, 0, 0)),
            blk, blk, row, row, vec, vec,
        ],
        out_specs=pl.BlockSpec((tm, D), lambda i, c: (i, 0), pipeline_mode=once),
        out_shape=jax.ShapeDtypeStruct((t, D), F32),
        scratch_shapes=[pltpu.VMEM((D, tm), F32), pltpu.VMEM((ec, tm), BF16)],
        compiler_params=_params(("parallel", "arbitrary")),
        name="peer_experts",
    )(x1t, x1, u_bf, vt_bf, r2, e2, n1, e1, ln_g.reshape(1, -1), ln_b.reshape(1, -1))


def _rope_tables(seq):
    half = HEAD_DIM // 2
    inv_freq = ROPE_THETA ** (-jnp.arange(half, dtype=F32) / half)
    ang = jnp.arange(seq, dtype=F32)[:, None] * inv_freq[None, :]
    cos, sin = jnp.cos(ang), jnp.sin(ang)
    reps = LANES // HEAD_DIM
    cos_t = jnp.tile(jnp.concatenate([cos, cos], axis=1), (1, reps))
    sin_t = jnp.tile(jnp.concatenate([-sin, sin], axis=1), (1, reps))
    scale = 1.0 / math.sqrt(HEAD_DIM)
    return (jnp.stack([cos_t * scale, cos_t, jnp.ones_like(cos_t)]),
            jnp.stack([sin_t * scale, sin_t, jnp.zeros_like(sin_t)]))


def _encoder_layer(x, w, tables):
    batch, seq, _ = x.shape
    x2 = x.reshape(batch * seq, D)
    qkv = _qkv_proj(x2, w["w_in"], tables[0], tables[1], batch, seq)
    hc = _glu_proj(x2, w["w_in"])
    attn_parts = [_dilated_attention(qkv[g], g) for g in range(N_GROUPS)]
    hcn = _conv_module(hc, w["conv_w"], w["conv_b"], w["conv_ln_g"], w["conv_ln_b"], batch, seq)
    attn = _attn_merge(attn_parts, batch, seq)
    x1, x1t = _merge_layer(x2, attn, hcn, w["w_in"], w["w_attn_o"], w["w_conv_o"], w["w_out"],
                           w["ln1_g"], w["ln1_b"])
    route = _peer_route(x1t, w["wq_t"], w["keys"])
    y = _peer_experts(x1t, x1, w["u"], w["v_t"], route, w["ln2_g"], w["ln2_b"])
    return y.reshape(batch, seq, D)


def kernel(x_prompt, x_sample, w_in, w_attn_o, conv_w, conv_b, conv_ln_g, conv_ln_b, w_conv_o, w_out,
           ln1_g, ln1_b, peer_w_q, peer_sub_keys, peer_u, peer_v, ln2_g, ln2_b):
    assert w_in.shape[0] == DEPTH == 1
    w = {
        "w_in": w_in[0].astype(BF16),
        "w_attn_o": w_attn_o[0].astype(BF16),
        "w_conv_o": w_conv_o[0].astype(BF16),
        "w_out": w_out[0].astype(BF16),
        "conv_w": conv_w[0], "conv_b": conv_b[0], "conv_ln_g": conv_ln_g[0], "conv_ln_b": conv_ln_b[0],
        "ln1_g": ln1_g[0], "ln1_b": ln1_b[0], "ln2_g": ln2_g[0], "ln2_b": ln2_b[0],
        "wq_t": peer_w_q[0].T.astype(BF16),
        "keys": peer_sub_keys[0].astype(BF16),
        "u": peer_u[0].astype(BF16),
        "v_t": peer_v[0].reshape(N_EXPERTS // EXPERT_CHUNK, EXPERT_CHUNK, D).transpose(0, 2, 1).astype(BF16),
    }
    tables = _rope_tables(x_prompt.shape[1])
    assert x_sample.shape[1] == x_prompt.shape[1]
    return (_encoder_layer(x_prompt, w, tables), _encoder_layer(x_sample, w, tables))
```

```python
import functools
import math

import jax
import jax.numpy as jnp
from jax import lax
from jax.experimental import pallas as pl
from jax.experimental.pallas import tpu as pltpu

D = 2048
HEAD_DIM = 64
HEADS = 8
GROUP_W = HEADS * HEAD_DIM
DILATED_GROUPS = ((128, 1), (512, 4), (2048, 16))
N_GROUPS = len(DILATED_GROUPS)
ATTN_W = N_GROUPS * GROUP_W
BLK = 64
ROPE_THETA = 10000.0
CONV_C = D // 2
CONV_K = 31
IN_COLS = 3 * ATTN_W + 2 * CONV_C + 2 * D
GLU_COL0 = 3 * ATTN_W
GATE_COL0 = GLU_COL0 + 2 * CONV_C
PEER_HEADS = 8
N_KEYS = 128
N_EXPERTS = N_KEYS * N_KEYS
TOPK = 16
LN_EPS = 1e-5
DEPTH = 1
ALPHA = (2.0 * DEPTH) ** 0.25

LANES = 128
SUBLANES = 8
BF16_ROWS = 2 * SUBLANES
GATE_ROWS_LIVE = 2
ACC_ROWS = 1024
TOKEN_COLS = 256
MXU_COLS = 256
EXPERT_CHUNK = 512
ATTN_ROWS = 128
VMEM_LIMIT = 56 * 1024 * 1024

F32 = jnp.float32
BF16 = jnp.bfloat16
NEG_INF = float("-inf")


def _params(semantics):
    return pltpu.CompilerParams(dimension_semantics=semantics, vmem_limit_bytes=VMEM_LIMIT)


def _layer_norm(y, g, b):
    mu = jnp.mean(y, axis=-1, keepdims=True)
    yc = y - mu
    var = jnp.mean(yc * yc, axis=-1, keepdims=True)
    return yc * lax.rsqrt(var + LN_EPS) * g + b


def _qkv_body(x_ref, w_ref, cos_ref, sin_ref, o0_ref, o1_ref, o2_ref, xb_ref, res_ref):
    @pl.when(pl.program_id(1) == 0)
    def _():
        xb_ref[...] = x_ref[...].astype(BF16)

    acc = jnp.dot(xb_ref[...], w_ref[...], preferred_element_type=F32)
    c = cos_ref[...]
    s = sin_ref[...]
    lane = lax.broadcasted_iota(jnp.int32, c.shape, 1)
    first_half = (lane % HEAD_DIM) < (HEAD_DIM // 2)
    for cb in range(ATTN_W // LANES):
        xc = acc[:, cb * LANES:(cb + 1) * LANES]
        partner = jnp.where(first_half,
                            pltpu.roll(xc, LANES - HEAD_DIM // 2, 1),
                            pltpu.roll(xc, HEAD_DIM // 2, 1))
        res_ref[cb] = xc * c + partner * s

    tm = res_ref.shape[1]
    per_group = GROUP_W // LANES
    for g, o_ref in enumerate((o0_ref, o1_ref, o2_ref)):
        d = DILATED_GROUPS[g][1]
        for r in range(d):
            for cb in range(per_group):
                rows = res_ref[g * per_group + cb, pl.ds(r, tm // d, stride=d), :]
                o_ref[r, :, cb * LANES:(cb + 1) * LANES] = rows.astype(o_ref.dtype)


def _qkv_proj(x2, w_in_bf, cos_t, sin_t, batch, seq):
    t = x2.shape[0]
    tm = 1024
    nseq = seq // tm
    out_specs, out_shape = [], []
    for _, d in DILATED_GROUPS:
        assert tm % (d * BF16_ROWS) == 0
        out_specs.append(pl.BlockSpec((None, d, tm // d, GROUP_W), lambda i, j: (i // nseq, 0, i % nseq, j)))
        out_shape.append(jax.ShapeDtypeStruct((batch, d, seq // d, 3 * GROUP_W), BF16))
    return pl.pallas_call(
        _qkv_body,
        grid=(t // tm, 3),
        in_specs=[
            pl.BlockSpec((tm, D), lambda i, j: (i, 0)),
            pl.BlockSpec((D, ATTN_W), lambda i, j: (0, j)),
            pl.BlockSpec((None, tm, LANES), lambda i, j: (j, i % nseq, 0)),
            pl.BlockSpec((None, tm, LANES), lambda i, j: (j, i % nseq, 0)),
        ],
        out_specs=out_specs,
        out_shape=out_shape,
        scratch_shapes=[pltpu.VMEM((tm, D), BF16), pltpu.VMEM((ATTN_W // LANES, tm, LANES), F32)],
        compiler_params=_params(("parallel", "arbitrary")),
        name="qkv_proj",
    )(x2, w_in_bf, cos_t, sin_t)


def _glu_body(x_ref, wa_ref, wb_ref, o_ref, xb_ref):
    @pl.when(pl.program_id(1) == 0)
    def _():
        xb_ref[...] = x_ref[...].astype(BF16)

    xb = xb_ref[...]
    for c0 in range(0, o_ref.shape[1], MXU_COLS):
        cs = slice(c0, c0 + MXU_COLS)
        a = jnp.dot(xb, wa_ref[:, cs], preferred_element_type=F32)
        b = jnp.dot(xb, wb_ref[:, cs], preferred_element_type=F32)
        o_ref[:, cs] = a * jax.nn.sigmoid(b)


def _glu_proj(x2, w_in_bf):
    t = x2.shape[0]
    tm, tn = 1024, 512
    a0 = GLU_COL0 // tn
    b0 = (GLU_COL0 + CONV_C) // tn
    return pl.pallas_call(
        _glu_body,
        grid=(t // tm, CONV_C // tn),
        in_specs=[
            pl.BlockSpec((tm, D), lambda i, j: (i, 0)),
            pl.BlockSpec((D, tn), lambda i, j: (0, a0 + j)),
            pl.BlockSpec((D, tn), lambda i, j: (0, b0 + j)),
        ],
        out_specs=pl.BlockSpec((tm, tn), lambda i, j: (i, j)),
        out_shape=jax.ShapeDtypeStruct((t, CONV_C), F32),
        scratch_shapes=[pltpu.VMEM((tm, D), BF16)],
        compiler_params=_params(("parallel", "arbitrary")),
        name="glu_proj",
    )(x2, w_in_bf, w_in_bf)


def _attn_body(q_ref, kp_ref, km_ref, kn_ref, vp_ref, vm_ref, vn_ref, o_ref, lse_ref, *, sub_len):
    n = pl.program_id(2)
    tq = q_ref.shape[0]
    k = jnp.concatenate([kp_ref[...], km_ref[...], kn_ref[...]], axis=0)
    v = jnp.concatenate([vp_ref[...], vm_ref[...], vn_ref[...]], axis=0)
    tk = ATTN_ROWS + 2 * BLK
    row = lax.broadcasted_iota(jnp.int32, (ATTN_ROWS, tk), 0)
    col = lax.broadcasted_iota(jnp.int32, (ATTN_ROWS, tk), 1)
    band = (col >= row) & (col <= row + 2 * BLK)
    heads = [slice(h * HEAD_DIM, (h + 1) * HEAD_DIM) for h in range(HEADS)]
    for r0 in range(0, tq, ATTN_ROWS):
        pos = n * tq + r0 - BLK + col
        valid = band & (pos >= 0) & (pos < sub_len)
        scores = [lax.dot_general(q_ref[r0:r0 + ATTN_ROWS, sl], k[r0:r0 + tk, sl], (((1,), (1,)), ((), ())),
                                  preferred_element_type=F32) for sl in heads]
        probs, stats = [], []
        for s in scores:
            s = jnp.where(valid, s, NEG_INF)
            m = jnp.max(s, axis=-1, keepdims=True)
            p = jnp.exp(s - m)
            l = jnp.sum(p, axis=-1, keepdims=True)
            probs.append(p.astype(BF16))
            stats.append((m, l))
        outs = [jnp.dot(p, v[r0:r0 + tk, sl], preferred_element_type=F32) for p, sl in zip(probs, heads)]
        for sl, o, (m, l) in zip(heads, outs, stats):
            o_ref[r0:r0 + ATTN_ROWS, sl] = o / l
            lse_ref[r0:r0 + ATTN_ROWS, sl] = jnp.broadcast_to(m + jnp.log(l), (ATTN_ROWS, HEAD_DIM))


def _dilated_attention(qkv_g, g):
    window, d = DILATED_GROUPS[g]
    batch, _, sub_len, _ = qkv_g.shape
    assert window // (2 * d) == BLK and sub_len % BLK == 0
    tq = min(256, sub_len)
    nb = sub_len // BLK

    def main(col):
        return pl.BlockSpec((None, None, tq, GROUP_W), lambda b, r, n: (b, r, n, col))

    def prev(col):
        return pl.BlockSpec((None, None, BLK, GROUP_W),
                            lambda b, r, n: (b, r, jnp.maximum(n * (tq // BLK) - 1, 0), col))

    def nxt(col):
        return pl.BlockSpec((None, None, BLK, GROUP_W),
                            lambda b, r, n: (b, r, jnp.minimum((n + 1) * (tq // BLK), nb - 1), col))

    return pl.pallas_call(
        functools.partial(_attn_body, sub_len=sub_len),
        grid=(batch, d, sub_len // tq),
        in_specs=[main(0), prev(1), main(1), nxt(1), prev(2), main(2), nxt(2)],
        out_specs=[main(0), main(0)],
        out_shape=[jax.ShapeDtypeStruct((batch, d, sub_len, GROUP_W), F32)] * 2,
        compiler_params=_params(("parallel", "parallel", "arbitrary")),
        name=f"dilated_attn_{g}",
    )(qkv_g, qkv_g, qkv_g, qkv_g, qkv_g, qkv_g, qkv_g)


CONV_HALO = 16
CONV_ROWS = 64


def _conv_body(hp_ref, hm_ref, hn_ref, w_ref, b_ref, g_ref, beta_ref, o_ref, win_ref, sh_ref, y_ref):
    n = pl.program_id(1)
    tc = hm_ref.shape[0]
    win_ref[0:CONV_HALO, :] = jnp.where(n > 0, hp_ref[...], 0.0)
    win_ref[CONV_HALO:CONV_HALO + tc, :] = hm_ref[...]
    win_ref[CONV_HALO + tc:, :] = jnp.where(n < pl.num_programs(1) - 1, hn_ref[...], 0.0)
    off = CONV_HALO - CONV_K // 2
    span = tc + CONV_HALO + SUBLANES
    for b in range(1, SUBLANES):
        sh_ref[b - 1] = win_ref[b:b + span, :]
    for cb in range(CONV_C // LANES):
        cs = slice(cb * LANES, (cb + 1) * LANES)
        for r0 in range(0, tc, CONV_ROWS):
            acc = jnp.zeros((CONV_ROWS, LANES), F32)
            for j in range(CONV_K):
                a, b = divmod(off + j, SUBLANES)
                rows = slice(r0 + a * SUBLANES, r0 + a * SUBLANES + CONV_ROWS)
                tap = win_ref[rows, cs] if b == 0 else sh_ref[b - 1, rows, cs]
                acc = acc + tap * w_ref[j:j + 1, cs]
            y_ref[r0:r0 + CONV_ROWS, cs] = acc
    y = _layer_norm(y_ref[...] + b_ref[...], g_ref[...], beta_ref[...])
    o_ref[...] = (y * jax.nn.sigmoid(y)).astype(o_ref.dtype)


def _conv_module(hc, conv_w, conv_b, ln_g, ln_b, batch, seq):
    tc = 256
    nh = seq // CONV_HALO
    hc3 = hc.reshape(batch, seq, CONV_C)
    vec = pl.BlockSpec((1, CONV_C), lambda b, n: (0, 0))
    out = pl.pallas_call(
        _conv_body,
        grid=(batch, seq // tc),
        in_specs=[
            pl.BlockSpec((None, CONV_HALO, CONV_C),
                         lambda b, n: (b, jnp.maximum(n * (tc // CONV_HALO) - 1, 0), 0)),
            pl.BlockSpec((None, tc, CONV_C), lambda b, n: (b, n, 0)),
            pl.BlockSpec((None, CONV_HALO, CONV_C),
                         lambda b, n: (b, jnp.minimum((n + 1) * (tc // CONV_HALO), nh - 1), 0)),
            pl.BlockSpec((CONV_K, CONV_C), lambda b, n: (0, 0)),
            vec, vec, vec,
        ],
        out_specs=pl.BlockSpec((None, tc, CONV_C), lambda b, n: (b, n, 0)),
        out_shape=jax.ShapeDtypeStruct((batch, seq, CONV_C), BF16),
        scratch_shapes=[pltpu.VMEM((tc + 2 * CONV_HALO, CONV_C), F32),
                        pltpu.VMEM((SUBLANES - 1, tc + CONV_HALO + SUBLANES, CONV_C), F32),
                        pltpu.VMEM((tc, CONV_C), F32)],
        compiler_params=_params(("parallel", "arbitrary")),
        name="conv_module",
    )(hc3, hc3, hc3, conv_w, conv_b.reshape(1, -1), ln_g.reshape(1, -1), ln_b.reshape(1, -1))
    return out.reshape(batch * seq, CONV_C)


def _attn_merge_body(o0_ref, l0_ref, o1_ref, l1_ref, o2_ref, l2_ref, attn_ref, *scratch):
    tm = attn_ref.shape[0]
    for cb in range(GROUP_W // LANES):
        cs = slice(cb * LANES, (cb + 1) * LANES)
        parts = [(o0_ref[0, :, cs], l0_ref[0, :, cs])]
        for g, (o_ref, l_ref) in enumerate(((o1_ref, l1_ref), (o2_ref, l2_ref)), start=1):
            d = DILATED_GROUPS[g][1]
            o_s, l_s = scratch[2 * (g - 1)], scratch[2 * (g - 1) + 1]
            for r in range(d):
                o_s[pl.ds(r, tm // d, stride=d), :] = o_ref[r, :, cs]
                l_s[pl.ds(r, tm // d, stride=d), :] = l_ref[r, :, cs]
            parts.append((o_s[...], l_s[...]))
        (o0, l0), (o1, l1), (o2, l2) = parts
        m = jnp.maximum(jnp.maximum(l0, l1), l2)
        e0, e1, e2 = jnp.exp(l0 - m), jnp.exp(l1 - m), jnp.exp(l2 - m)
        attn = (e0 * o0 + e1 * o1 + e2 * o2) / (e0 + e1 + e2)
        attn_ref[:, cs] = attn.astype(attn_ref.dtype)


def _attn_merge(attn_parts, batch, seq):
    tm = 512
    in_specs, args = [], []
    for g, (o, l) in enumerate(attn_parts):
        d = DILATED_GROUPS[g][1]
        assert tm % (d * SUBLANES) == 0
        spec = pl.BlockSpec((None, d, tm // d, GROUP_W), lambda b, n: (b, 0, n, 0))
        in_specs += [spec, spec]
        args += [o, l]
    attn = pl.pallas_call(
        _attn_merge_body,
        grid=(batch, seq // tm),
        in_specs=in_specs,
        out_specs=pl.BlockSpec((None, tm, GROUP_W), lambda b, n: (b, n, 0)),
        out_shape=jax.ShapeDtypeStruct((batch, seq, GROUP_W), BF16),
        scratch_shapes=[pltpu.VMEM((tm, LANES), F32)] * 4,
        compiler_params=_params(("parallel", "parallel")),
        name="attn_merge",
    )(*args)
    return attn.reshape(batch * seq, GROUP_W)


def _merge_body(x_ref, attn_ref, hcn_ref, wga_ref, wgb_ref, wao_ref, wco_ref, wout_ref, g_ref, b_ref,
                x1_ref, x1t_ref, xb_ref, acc_ref):
    j = pl.program_id(1)

    @pl.when(j == 0)
    def _():
        xb_ref[...] = x_ref[...].astype(BF16)
        acc_ref[...] = jnp.zeros_like(acc_ref)

    xb = xb_ref[...]
    ga = jax.nn.sigmoid(jnp.dot(xb, wga_ref[...], preferred_element_type=F32))
    gb = jax.nn.sigmoid(jnp.dot(xb, wgb_ref[...], preferred_element_type=F32))
    br_a = jnp.dot(attn_ref[...], wao_ref[...], preferred_element_type=F32)
    br_b = jnp.dot(hcn_ref[...], wco_ref[...], preferred_element_type=F32)
    merged = (ga * br_a + gb * br_b).astype(BF16)
    acc_ref[...] += jnp.dot(merged, wout_ref[...], preferred_element_type=F32)

    @pl.when(j == pl.num_programs(1) - 1)
    def _():
        x1 = _layer_norm(ALPHA * x_ref[...] + acc_ref[...], g_ref[...], b_ref[...])
        x1_ref[...] = x1
        x1t_ref[...] = x1.T.astype(BF16)


def _merge_layer(x2, attn, hcn, w_in_bf, w_ao_bf, w_co_bf, w_out_bf, ln_g, ln_b):
    t = x2.shape[0]
    tm, tn = 512, 512
    nj = D // tn
    ga0 = GATE_COL0 // tn
    gb0 = (GATE_COL0 + D) // tn
    vec = pl.BlockSpec((1, D), lambda i, j: (0, 0))
    return pl.pallas_call(
        _merge_body,
        grid=(t // tm, nj),
        in_specs=[
            pl.BlockSpec((tm, D), lambda i, j: (i, 0)),
            pl.BlockSpec((tm, GROUP_W), lambda i, j: (i, 0)),
            pl.BlockSpec((tm, CONV_C), lambda i, j: (i, 0)),
            pl.BlockSpec((D, tn), lambda i, j: (0, ga0 + j)),
            pl.BlockSpec((D, tn), lambda i, j: (0, gb0 + j)),
            pl.BlockSpec((GROUP_W, tn), lambda i, j: (0, j)),
            pl.BlockSpec((CONV_C, tn), lambda i, j: (0, j)),
            pl.BlockSpec((tn, D), lambda i, j: (j, 0)),
            vec, vec,
        ],
        out_specs=[pl.BlockSpec((tm, D), lambda i, j: (i, 0)),
                   pl.BlockSpec((D, tm), lambda i, j: (0, i))],
        out_shape=[jax.ShapeDtypeStruct((t, D), F32), jax.ShapeDtypeStruct((D, t), BF16)],
        scratch_shapes=[pltpu.VMEM((tm, D), BF16), pltpu.VMEM((tm, D), F32)],
        compiler_params=_params(("parallel", "arbitrary")),
        name="merge_layer",
    )(x2, attn, hcn, w_in_bf, w_in_bf, w_ao_bf, w_co_bf, w_out_bf, ln_g.reshape(1, -1), ln_b.reshape(1, -1))


RANK_CODE_BASE = -(2.0 ** 120)


def _top16(s, exact_ties):
    row = lax.broadcasted_iota(jnp.int32, s.shape, 0).astype(F32)
    vals = []
    for k in range(TOPK):
        m = jnp.max(s, axis=0, keepdims=True)
        hit = s == m
        if exact_ties:
            hit = row == jnp.min(jnp.where(hit, row, float(N_KEYS)), axis=0, keepdims=True)
        s = jnp.where(hit, RANK_CODE_BASE * (1.0 + k / TOPK), s)
        vals.append(m)
    taken = s <= RANK_CODE_BASE
    rank = jnp.where(taken, (s * (1.0 / RANK_CODE_BASE) - 1.0) * TOPK, float(TOPK))
    n_taken = jnp.sum(jnp.where(taken, 1.0, 0.0), axis=0, keepdims=True)
    return rank, jnp.concatenate(vals, axis=0), n_taken


def _pair_counts(a1, a2, exact_ties):
    assert TOPK == 16 and SUBLANES == 8
    row = lax.broadcasted_iota(jnp.int32, (SUBLANES, LANES), 0)
    lo, hi = a1[0:SUBLANES, :], a1[SUBLANES:, :]
    first = a1[0:1, :]
    tail = jnp.roll(a2[SUBLANES:, :], 2, 0)
    blocks = [
        lo + a2[0:1, :],
        hi + a2[0:1, :],
        lo + a2[1:2, :],
        jnp.where(row < 5, lo + a2[2:3, :], jnp.roll(lo, 5, 0) + a2[4:5, :]),
        jnp.where(row < 4, lo + a2[3:4, :],
                  jnp.where(row < 6, jnp.roll(lo, 4, 0) + a2[5:6, :], jnp.roll(lo, 6, 0) + a2[6:7, :])),
        jnp.where(row < 2, lo + a2[7:8, :], first + tail),
        jnp.where(row < 2, first + tail, NEG_INF),
    ]
    ra = [row, row + SUBLANES, row, jnp.where(row < 5, row, row - 5),
          jnp.where(row < 4, row, jnp.where(row < 6, row - 4, row - 6)),
          jnp.where(row < 2, row, 0), jnp.zeros_like(row)]
    rb = [jnp.zeros_like(row), jnp.zeros_like(row), jnp.ones_like(row), jnp.where(row < 5, 2, 4),
          jnp.where(row < 4, 3, jnp.where(row < 6, 5, 6)),
          jnp.where(row < 2, 7, row + 6), row + 14]
    cand0 = jnp.concatenate(blocks, axis=0)
    pos = jnp.concatenate([a * TOPK + b for a, b in zip(ra, rb)], axis=0).astype(F32)
    top = a1[0:1, :] + a2[0:1, :]
    cand = cand0
    for _ in range(TOPK):
        m = jnp.max(cand, axis=0, keepdims=True)
        hit = cand == m
        if exact_ties:
            hit = pos == jnp.min(jnp.where(hit, pos, float(TOPK * TOPK)), axis=0, keepdims=True)
        cand = jnp.where(hit, NEG_INF, cand)
    chosen = cand != cand0
    z = jnp.sum(jnp.where(chosen, jnp.exp(cand0 - top), 0.0), axis=0, keepdims=True)
    cnt = [jnp.where(chosen[k * SUBLANES:(k + 1) * SUBLANES, :], 1.0, 0.0) for k in range(len(blocks))]
    extra = (jnp.sum(jnp.where(row >= 2, cnt[5], 0.0), axis=0, keepdims=True)
             + jnp.sum(cnt[6], axis=0, keepdims=True))
    n_lo = (cnt[0] + cnt[2]
            + jnp.where(row < 5, cnt[3], 0.0) + jnp.roll(jnp.where(row >= 5, cnt[3], 0.0), 3, 0)
            + jnp.where(row < 4, cnt[4], 0.0)
            + jnp.roll(jnp.where((row >= 4) & (row < 6), cnt[4], 0.0), 4, 0)
            + jnp.roll(jnp.where(row >= 6, cnt[4], 0.0), 2, 0)
            + jnp.where(row < 2, cnt[5], 0.0)
            + jnp.where(row == 0, extra, 0.0))
    n = jnp.concatenate([n_lo, cnt[1]], axis=0)
    return n, z, jnp.sum(n, axis=0, keepdims=True)


def _route_body(x1t_ref, wq_ref, keys_ref, r2_ref, e2_ref, n1_ref, e1_ref, q_ref, sc_ref):
    tm = x1t_ref.shape[1]
    ns = tm // LANES
    q_ref[...] = jnp.dot(wq_ref[...], x1t_ref[...], preferred_element_type=F32).astype(BF16)
    for hp in range(2 * PEER_HEADS):
        st = jnp.dot(keys_ref[hp % 2], q_ref[hp * N_KEYS:(hp + 1) * N_KEYS, :],
                     preferred_element_type=F32)
        for sl in range(ns):
            sc_ref[hp // 2, sl, hp % 2] = st[:, sl * LANES:(sl + 1) * LANES]

    def per_head_slab(idx, carry):
        h = idx // ns
        sl = idx % ns

        def route(exact_ties):
            s1 = sc_ref[h, sl, 0]
            s2 = sc_ref[h, sl, 1]
            r1, a1, t1 = _top16(s1, exact_ties)
            r2, a2, t2 = _top16(s2, exact_ties)
            n, z, tc = _pair_counts(a1, a2, exact_ties)
            n1 = jnp.zeros_like(r1)
            for a in range(TOPK):
                n1 = jnp.where(r1 == float(a), n[a:a + 1, :], n1)
            e1 = jnp.where(r1 < float(TOPK), jnp.exp(s1 - a1[0:1, :]), 0.0)
            e2 = jnp.where(r2 < float(TOPK), jnp.exp(s2 - a2[0:1, :]), 0.0) / z
            r2_ref[sl, h] = r2.astype(r2_ref.dtype)
            e2_ref[sl, h] = e2.astype(e2_ref.dtype)
            n1_ref[sl, h] = n1
            e1_ref[sl, h] = e1
            return (t1 == float(TOPK)) & (t2 == float(TOPK)) & (tc == float(TOPK))

        ok = route(False)

        @pl.when(jnp.max(jnp.where(ok, 0.0, 1.0)) > 0.0)
        def _():
            route(True)

        return carry

    lax.fori_loop(0, PEER_HEADS * ns, per_head_slab, 0)


def _peer_route(x1t, wq_t_bf, keys_bf):
    t = x1t.shape[1]
    tm = 512
    ns = tm // LANES
    blk = pl.BlockSpec((ns, PEER_HEADS, N_KEYS, LANES), lambda i: (i, 0, 0, 0))
    shape = (t // LANES, PEER_HEADS, N_KEYS, LANES)
    return pl.pallas_call(
        _route_body,
        grid=(t // tm,),
        in_specs=[
            pl.BlockSpec((D, tm), lambda i: (0, i)),
            pl.BlockSpec((D, D), lambda i: (0, 0)),
            pl.BlockSpec((2, N_KEYS, N_KEYS), lambda i: (0, 0, 0)),
        ],
        out_specs=[blk, blk, blk, blk],
        out_shape=[jax.ShapeDtypeStruct(shape, BF16), jax.ShapeDtypeStruct(shape, BF16),
                   jax.ShapeDtypeStruct(shape, F32), jax.ShapeDtypeStruct(shape, F32)],
        scratch_shapes=[pltpu.VMEM((D, tm), BF16), pltpu.VMEM((PEER_HEADS, ns, 2, N_KEYS, LANES), F32)],
        compiler_params=_params(("parallel",)),
        name="peer_route",
    )(x1t, wq_t_bf, keys_bf)


def _experts_body(x1t_ref, x1_ref, u_ref, vt_ref, r2_ref, e2_ref, n1_ref, e1_ref, g_ref, b_ref,
                  o_ref, acc_ref, gh_ref):
    c = pl.program_id(1)
    tm = x1t_ref.shape[1]
    ns = tm // LANES
    ni = u_ref.shape[0] // N_KEYS

    @pl.when(c == 0)
    def _():
        acc_ref[...] = jnp.zeros_like(acc_ref)

    nb = N_KEYS // BF16_ROWS
    zero = jnp.zeros((BF16_ROWS, LANES), BF16)
    for i0 in range(0, ni, GATE_ROWS_LIVE):
        for ts in range(ns):
            cs = slice(ts * LANES, (ts + 1) * LANES)
            gate = [[None] * nb for _ in range(GATE_ROWS_LIVE)]
            for h in range(PEER_HEADS):
                r2 = [r2_ref[ts, h, jb * BF16_ROWS:(jb + 1) * BF16_ROWS, :] for jb in range(nb)]
                e2 = [e2_ref[ts, h, jb * BF16_ROWS:(jb + 1) * BF16_ROWS, :] for jb in range(nb)]
                for ii in range(GATE_ROWS_LIVE):
                    i = (c % (SUBLANES // ni)) * ni + i0 + ii
                    n1 = jnp.broadcast_to(n1_ref[ts, h, pl.ds(i, 1), :], (BF16_ROWS, LANES)).astype(BF16)
                    e1 = jnp.broadcast_to(e1_ref[ts, h, pl.ds(i, 1), :], (BF16_ROWS, LANES)).astype(BF16)
                    for jb in range(nb):
                        term = jnp.minimum(jnp.maximum(n1 - r2[jb], zero), e1) * e2[jb]
                        gate[ii][jb] = term if h == 0 else gate[ii][jb] + term
            for ii in range(GATE_ROWS_LIVE):
                for jb in range(nb):
                    r0 = (i0 + ii) * N_KEYS + jb * BF16_ROWS
                    gh_ref[r0:r0 + BF16_ROWS, cs] = gate[ii][jb]
    for t0 in range(0, tm, TOKEN_COLS):
        cs = slice(t0, t0 + TOKEN_COLS)
        ht = jnp.dot(u_ref[...], x1t_ref[:, cs], preferred_element_type=F32)
        act = (0.5 * ht * (1.0 + lax.erf(ht * (1.0 / math.sqrt(2.0))))).astype(BF16)
        gh_ref[:, cs] = gh_ref[:, cs] * act
    for t0 in range(0, tm, TOKEN_COLS):
        cs = slice(t0, t0 + TOKEN_COLS)
        for r0 in range(0, D, ACC_ROWS):
            acc_ref[r0:r0 + ACC_ROWS, cs] += jnp.dot(vt_ref[r0:r0 + ACC_ROWS, :], gh_ref[:, cs],
                                                     preferred_element_type=F32)

    @pl.when(c == pl.num_programs(1) - 1)
    def _():
        y = ALPHA * x1_ref[...] + acc_ref[...].T
        o_ref[...] = _layer_norm(y, g_ref[...], b_ref[...])


def _peer_experts(x1t, x1, u_bf, vt_bf, route, ln_g, ln_b):
    t = x1.shape[0]
    tm, ec = 1024, EXPERT_CHUNK
    ns = tm // LANES
    r2, e2, n1, e1 = route
    ni = ec // N_KEYS
    assert SUBLANES % ni == 0
    once = pl.Buffered(1)
    blk = pl.BlockSpec((ns, PEER_HEADS, N_KEYS, LANES), lambda i, c: (i, 0, 0, 0), pipeline_mode=once)
    row = pl.BlockSpec((ns, PEER_HEADS, SUBLANES, LANES), lambda i, c: (i, 0, c // (SUBLANES // ni), 0))
    vec = pl.BlockSpec((1, D), lambda i, c: (0, 0))
    return pl.pallas_call(
        _experts_body,
        grid=(t // tm, N_EXPERTS // ec),
        in_specs=[
            pl.BlockSpec((D, tm), lambda i, c: (0, i)),
            pl.BlockSpec((tm, D), lambda i, c: (i, 0), pipeline_mode=once),
            pl.BlockSpec((ec, D), lambda i, c: (c, 0)),
            pl.BlockSpec((None, D, ec), lambda i, c: (c, 0, 0)),
            blk, blk, row, row, vec, vec,
        ],
        out_specs=pl.BlockSpec((tm, D), lambda i, c: (i, 0), pipeline_mode=once),
        out_shape=jax.ShapeDtypeStruct((t, D), F32),
        scratch_shapes=[pltpu.VMEM((D, tm), F32), pltpu.VMEM((ec, tm), BF16)],
        compiler_params=_params(("parallel", "arbitrary")),
        name="peer_experts",
    )(x1t, x1, u_bf, vt_bf, r2, e2, n1, e1, ln_g.reshape(1, -1), ln_b.reshape(1, -1))


def _rope_tables(seq):
    half = HEAD_DIM // 2
    inv_freq = ROPE_THETA ** (-jnp.arange(half, dtype=F32) / half)
    ang = jnp.arange(seq, dtype=F32)[:, None] * inv_freq[None, :]
    cos, sin = jnp.cos(ang), jnp.sin(ang)
    reps = LANES // HEAD_DIM
    cos_t = jnp.tile(jnp.concatenate([cos, cos], axis=1), (1, reps))
    sin_t = jnp.tile(jnp.concatenate([-sin, sin], axis=1), (1, reps))
    scale = 1.0 / math.sqrt(HEAD_DIM)
    return (jnp.stack([cos_t * scale, cos_t, jnp.ones_like(cos_t)]),
            jnp.stack([sin_t * scale, sin_t, jnp.zeros_like(sin_t)]))


def _encoder_layer(x, w, tables):
    batch, seq, _ = x.shape
    x2 = x.reshape(batch * seq, D)
    qkv = _qkv_proj(x2, w["w_in"], tables[0], tables[1], batch, seq)
    hc = _glu_proj(x2, w["w_in"])
    attn_parts = [_dilated_attention(qkv[g], g) for g in range(N_GROUPS)]
    hcn = _conv_module(hc, w["conv_w"], w["conv_b"], w["conv_ln_g"], w["conv_ln_b"], batch, seq)
    attn = _attn_merge(attn_parts, batch, seq)
    x1, x1t = _merge_layer(x2, attn, hcn, w["w_in"], w["w_attn_o"], w["w_conv_o"], w["w_out"],
                           w["ln1_g"], w["ln1_b"])
    route = _peer_route(x1t, w["wq_t"], w["keys"])
    y = _peer_experts(x1t, x1, w["u"], w["v_t"], route, w["ln2_g"], w["ln2_b"])
    return y.reshape(batch, seq, D)


def kernel(x_prompt, x_sample, w_in, w_attn_o, conv_w, conv_b, conv_ln_g, conv_ln_b, w_conv_o, w_out,
           ln1_g, ln1_b, peer_w_q, peer_sub_keys, peer_u, peer_v, ln2_g, ln2_b):
    assert w_in.shape[0] == DEPTH == 1
    w = {
        "w_in": w_in[0].astype(BF16),
        "w_attn_o": w_attn_o[0].astype(BF16),
        "w_conv_o": w_conv_o[0].astype(BF16),
        "w_out": w_out[0].astype(BF16),
        "conv_w": conv_w[0], "conv_b": conv_b[0], "conv_ln_g": conv_ln_g[0], "conv_ln_b": conv_ln_b[0],
        "ln1_g": ln1_g[0], "ln1_b": ln1_b[0], "ln2_g": ln2_g[0], "ln2_b": ln2_b[0],
        "wq_t": peer_w_q[0].T.astype(BF16),
        "keys": peer_sub_keys[0].astype(BF16),
        "u": peer_u[0].astype(BF16),
        "v_t": peer_v[0].reshape(N_EXPERTS // EXPERT_CHUNK, EXPERT_CHUNK, D).transpose(0, 2, 1).astype(BF16),
    }
    tables = _rope_tables(x_prompt.shape[1])
    assert x_sample.shape[1] == x_prompt.shape[1]
    return (_encoder_layer(x_prompt, w, tables), _encoder_layer(x_sample, w, tables))
```

```python
import functools
import math

import jax
import jax.numpy as jnp
from jax import lax
from jax.experimental import pallas as pl
from jax.experimental.pallas import tpu as pltpu

D = 2048
HEAD_DIM = 64
HEADS = 8
GROUP_W = HEADS * HEAD_DIM
DILATED_GROUPS = ((128, 1), (512, 4), (2048, 16))
N_GROUPS = len(DILATED_GROUPS)
ATTN_W = N_GROUPS * GROUP_W
BLK = 64
ROPE_THETA = 10000.0
CONV_C = D // 2
CONV_K = 31
IN_COLS = 3 * ATTN_W + 2 * CONV_C + 2 * D
GLU_COL0 = 3 * ATTN_W
GATE_COL0 = GLU_COL0 + 2 * CONV_C
PEER_HEADS = 8
N_KEYS = 128
N_EXPERTS = N_KEYS * N_KEYS
TOPK = 16
LN_EPS = 1e-5
DEPTH = 1
ALPHA = (2.0 * DEPTH) ** 0.25

LANES = 128
SUBLANES = 8
BF16_ROWS = 2 * SUBLANES
GATE_ROWS_LIVE = 2
ACC_ROWS = 1024
TOKEN_COLS = 256
MXU_COLS = 256
EXPERT_CHUNK = 512
ATTN_ROWS = 128
VMEM_LIMIT = 56 * 1024 * 1024

F32 = jnp.float32
BF16 = jnp.bfloat16
NEG_INF = float("-inf")


def _params(semantics):
    return pltpu.CompilerParams(dimension_semantics=semantics, vmem_limit_bytes=VMEM_LIMIT)


def _layer_norm(y, g, b):
    mu = jnp.mean(y, axis=-1, keepdims=True)
    yc = y - mu
    var = jnp.mean(yc * yc, axis=-1, keepdims=True)
    return yc * lax.rsqrt(var + LN_EPS) * g + b


def _qkv_body(x_ref, w_ref, cos_ref, sin_ref, o0_ref, o1_ref, o2_ref, xb_ref, res_ref):
    @pl.when(pl.program_id(1) == 0)
    def _():
        xb_ref[...] = x_ref[...].astype(BF16)

    acc = jnp.dot(xb_ref[...], w_ref[...], preferred_element_type=F32)
    c = cos_ref[...]
    s = sin_ref[...]
    lane = lax.broadcasted_iota(jnp.int32, c.shape, 1)
    first_half = (lane % HEAD_DIM) < (HEAD_DIM // 2)
    for cb in range(ATTN_W // LANES):
        xc = acc[:, cb * LANES:(cb + 1) * LANES]
        partner = jnp.where(first_half,
                            pltpu.roll(xc, LANES - HEAD_DIM // 2, 1),
                            pltpu.roll(xc, HEAD_DIM // 2, 1))
        res_ref[cb] = xc * c + partner * s

    tm = res_ref.shape[1]
    per_group = GROUP_W // LANES
    for g, o_ref in enumerate((o0_ref, o1_ref, o2_ref)):
        d = DILATED_GROUPS[g][1]
        for r in range(d):
            for cb in range(per_group):
                rows = res_ref[g * per_group + cb, pl.ds(r, tm // d, stride=d), :]
                o_ref[r, :, cb * LANES:(cb + 1) * LANES] = rows.astype(o_ref.dtype)


def _qkv_proj(x2, w_in_bf, cos_t, sin_t, batch, seq):
    t = x2.shape[0]
    tm = 1024
    nseq = seq // tm
    out_specs, out_shape = [], []
    for _, d in DILATED_GROUPS:
        assert tm % (d * BF16_ROWS) == 0
        out_specs.append(pl.BlockSpec((None, d, tm // d, GROUP_W), lambda i, j: (i // nseq, 0, i % nseq, j)))
        out_shape.append(jax.ShapeDtypeStruct((batch, d, seq // d, 3 * GROUP_W), BF16))
    return pl.pallas_call(
        _qkv_body,
        grid=(t // tm, 3),
        in_specs=[
            pl.BlockSpec((tm, D), lambda i, j: (i, 0)),
            pl.BlockSpec((D, ATTN_W), lambda i, j: (0, j)),
            pl.BlockSpec((None, tm, LANES), lambda i, j: (j, i % nseq, 0)),
            pl.BlockSpec((None, tm, LANES), lambda i, j: (j, i % nseq, 0)),
        ],
        out_specs=out_specs,
        out_shape=out_shape,
        scratch_shapes=[pltpu.VMEM((tm, D), BF16), pltpu.VMEM((ATTN_W // LANES, tm, LANES), F32)],
        compiler_params=_params(("parallel", "arbitrary")),
        name="qkv_proj",
    )(x2, w_in_bf, cos_t, sin_t)


def _glu_body(x_ref, wa_ref, wb_ref, o_ref, xb_ref):
    @pl.when(pl.program_id(1) == 0)
    def _():
        xb_ref[...] = x_ref[...].astype(BF16)

    xb = xb_ref[...]
    for c0 in range(0, o_ref.shape[1], MXU_COLS):
        cs = slice(c0, c0 + MXU_COLS)
        a = jnp.dot(xb, wa_ref[:, cs], preferred_element_type=F32)
        b = jnp.dot(xb, wb_ref[:, cs], preferred_element_type=F32)
        o_ref[:, cs] = a * jax.nn.sigmoid(b)


def _glu_proj(x2, w_in_bf):
    t = x2.shape[0]
    tm, tn = 1024, 512
    a0 = GLU_COL0 // tn
    b0 = (GLU_COL0 + CONV_C) // tn
    return pl.pallas_call(
        _glu_body,
        grid=(t // tm, CONV_C // tn),
        in_specs=[
            pl.BlockSpec((tm, D), lambda i, j: (i, 0)),
            pl.BlockSpec((D, tn), lambda i, j: (0, a0 + j)),
            pl.BlockSpec((D, tn), lambda i, j: (0, b0 + j)),
        ],
        out_specs=pl.BlockSpec((tm, tn), lambda i, j: (i, j)),
        out_shape=jax.ShapeDtypeStruct((t, CONV_C), F32),
        scratch_shapes=[pltpu.VMEM((tm, D), BF16)],
        compiler_params=_params(("parallel", "arbitrary")),
        name="glu_proj",
    )(x2, w_in_bf, w_in_bf)


def _attn_body(q_ref, kp_ref, km_ref, kn_ref, vp_ref, vm_ref, vn_ref, o_ref, lse_ref, *, sub_len):
    n = pl.program_id(2)
    tq = q_ref.shape[0]
    k = jnp.concatenate([kp_ref[...], km_ref[...], kn_ref[...]], axis=0)
    v = jnp.concatenate([vp_ref[...], vm_ref[...], vn_ref[...]], axis=0)
    tk = ATTN_ROWS + 2 * BLK
    row = lax.broadcasted_iota(jnp.int32, (ATTN_ROWS, tk), 0)
    col = lax.broadcasted_iota(jnp.int32, (ATTN_ROWS, tk), 1)
    band = (col >= row) & (col <= row + 2 * BLK)
    heads = [slice(h * HEAD_DIM, (h + 1) * HEAD_DIM) for h in range(HEADS)]
    for r0 in range(0, tq, ATTN_ROWS):
        pos = n * tq + r0 - BLK + col
        valid = band & (pos >= 0) & (pos < sub_len)
        scores = [lax.dot_general(q_ref[r0:r0 + ATTN_ROWS, sl], k[r0:r0 + tk, sl], (((1,), (1,)), ((), ())),
                                  preferred_element_type=F32) for sl in heads]
        probs, stats = [], []
        for s in scores:
            s = jnp.where(valid, s, NEG_INF)
            m = jnp.max(s, axis=-1, keepdims=True)
            p = jnp.exp(s - m)
            l = jnp.sum(p, axis=-1, keepdims=True)
            probs.append(p.astype(BF16))
            stats.append((m, l))
        outs = [jnp.dot(p, v[r0:r0 + tk, sl], preferred_element_type=F32) for p, sl in zip(probs, heads)]
        for sl, o, (m, l) in zip(heads, outs, stats):
            o_ref[r0:r0 + ATTN_ROWS, sl] = o / l
            lse_ref[r0:r0 + ATTN_ROWS, sl] = jnp.broadcast_to(m + jnp.log(l), (ATTN_ROWS, HEAD_DIM))


def _dilated_attention(qkv_g, g):
    window, d = DILATED_GROUPS[g]
    batch, _, sub_len, _ = qkv_g.shape
    assert window // (2 * d) == BLK and sub_len % BLK == 0
    tq = min(256, sub_len)
    nb = sub_len // BLK

    def main(col):
        return pl.BlockSpec((None, None, tq, GROUP_W), lambda b, r, n: (b, r, n, col))

    def prev(col):
        return pl.BlockSpec((None, None, BLK, GROUP_W),
                            lambda b, r, n: (b, r, jnp.maximum(n * (tq // BLK) - 1, 0), col))

    def nxt(col):
        return pl.BlockSpec((None, None, BLK, GROUP_W),
                            lambda b, r, n: (b, r, jnp.minimum((n + 1) * (tq // BLK), nb - 1), col))

    return pl.pallas_call(
        functools.partial(_attn_body, sub_len=sub_len),
        grid=(batch, d, sub_len // tq),
        in_specs=[main(0), prev(1), main(1), nxt(1), prev(2), main(2), nxt(2)],
        out_specs=[main(0), main(0)],
        out_shape=[jax.ShapeDtypeStruct((batch, d, sub_len, GROUP_W), F32)] * 2,
        compiler_params=_params(("parallel", "parallel", "arbitrary")),
        name=f"dilated_attn_{g}",
    )(qkv_g, qkv_g, qkv_g, qkv_g, qkv_g, qkv_g, qkv_g)


CONV_HALO = 16
CONV_ROWS = 64


def _conv_body(hp_ref, hm_ref, hn_ref, w_ref, b_ref, g_ref, beta_ref, o_ref, win_ref, sh_ref, y_ref):
    n = pl.program_id(1)
    tc = hm_ref.shape[0]
    win_ref[0:CONV_HALO, :] = jnp.where(n > 0, hp_ref[...], 0.0)
    win_ref[CONV_HALO:CONV_HALO + tc, :] = hm_ref[...]
    win_ref[CONV_HALO + tc:, :] = jnp.where(n < pl.num_programs(1) - 1, hn_ref[...], 0.0)
    off = CONV_HALO - CONV_K // 2
    span = tc + CONV_HALO + SUBLANES
    for b in range(1, SUBLANES):
        sh_ref[b - 1] = win_ref[b:b + span, :]
    for cb in range(CONV_C // LANES):
        cs = slice(cb * LANES, (cb + 1) * LANES)
        for r0 in range(0, tc, CONV_ROWS):
            acc = jnp.zeros((CONV_ROWS, LANES), F32)
            for j in range(CONV_K):
                a, b = divmod(off + j, SUBLANES)
                rows = slice(r0 + a * SUBLANES, r0 + a * SUBLANES + CONV_ROWS)
                tap = win_ref[rows, cs] if b == 0 else sh_ref[b - 1, rows, cs]
                acc = acc + tap * w_ref[j:j + 1, cs]
            y_ref[r0:r0 + CONV_ROWS, cs] = acc
    y = _layer_norm(y_ref[...] + b_ref[...], g_ref[...], beta_ref[...])
    o_ref[...] = (y * jax.nn.sigmoid(y)).astype(o_ref.dtype)


def _conv_module(hc, conv_w, conv_b, ln_g, ln_b, batch, seq):
    tc = 256
    nh = seq // CONV_HALO
    hc3 = hc.reshape(batch, seq, CONV_C)
    vec = pl.BlockSpec((1, CONV_C), lambda b, n: (0, 0))
    out = pl.pallas_call(
        _conv_body,
        grid=(batch, seq // tc),
        in_specs=[
            pl.BlockSpec((None, CONV_HALO, CONV_C),
                         lambda b, n: (b, jnp.maximum(n * (tc // CONV_HALO) - 1, 0), 0)),
            pl.BlockSpec((None, tc, CONV_C), lambda b, n: (b, n, 0)),
            pl.BlockSpec((None, CONV_HALO, CONV_C),
                         lambda b, n: (b, jnp.minimum((n + 1) * (tc // CONV_HALO), nh - 1), 0)),
            pl.BlockSpec((CONV_K, CONV_C), lambda b, n: (0, 0)),
            vec, vec, vec,
        ],
        out_specs=pl.BlockSpec((None, tc, CONV_C), lambda b, n: (b, n, 0)),
        out_shape=jax.ShapeDtypeStruct((batch, seq, CONV_C), BF16),
        scratch_shapes=[pltpu.VMEM((tc + 2 * CONV_HALO, CONV_C), F32),
                        pltpu.VMEM((SUBLANES - 1, tc + CONV_HALO + SUBLANES, CONV_C), F32),
                        pltpu.VMEM((tc, CONV_C), F32)],
        compiler_params=_params(("parallel", "arbitrary")),
        name="conv_module",
    )(hc3, hc3, hc3, conv_w, conv_b.reshape(1, -1), ln_g.reshape(1, -1), ln_b.reshape(1, -1))
    return out.reshape(batch * seq, CONV_C)


def _attn_merge_body(o0_ref, l0_ref, o1_ref, l1_ref, o2_ref, l2_ref, attn_ref, *scratch):
    tm = attn_ref.shape[0]
    for cb in range(GROUP_W // LANES):
        cs = slice(cb * LANES, (cb + 1) * LANES)
        parts = [(o0_ref[0, :, cs], l0_ref[0, :, cs])]
        for g, (o_ref, l_ref) in enumerate(((o1_ref, l1_ref), (o2_ref, l2_ref)), start=1):
            d = DILATED_GROUPS[g][1]
            o_s, l_s = scratch[2 * (g - 1)], scratch[2 * (g - 1) + 1]
            for r in range(d):
                o_s[pl.ds(r, tm // d, stride=d), :] = o_ref[r, :, cs]
                l_s[pl.ds(r, tm // d, stride=d), :] = l_ref[r, :, cs]
            parts.append((o_s[...], l_s[...]))
        (o0, l0), (o1, l1), (o2, l2) = parts
        m = jnp.maximum(jnp.maximum(l0, l1), l2)
        e0, e1, e2 = jnp.exp(l0 - m), jnp.exp(l1 - m), jnp.exp(l2 - m)
        attn = (e0 * o0 + e1 * o1 + e2 * o2) / (e0 + e1 + e2)
        attn_ref[:, cs] = attn.astype(attn_ref.dtype)


def _attn_merge(attn_parts, batch, seq):
    tm = 512
    in_specs, args = [], []
    for g, (o, l) in enumerate(attn_parts):
        d = DILATED_GROUPS[g][1]
        assert tm % (d * SUBLANES) == 0
        spec = pl.BlockSpec((None, d, tm // d, GROUP_W), lambda b, n: (b, 0, n, 0))
        in_specs += [spec, spec]
        args += [o, l]
    attn = pl.pallas_call(
        _attn_merge_body,
        grid=(batch, seq // tm),
        in_specs=in_specs,
        out_specs=pl.BlockSpec((None, tm, GROUP_W), lambda b, n: (b, n, 0)),
        out_shape=jax.ShapeDtypeStruct((batch, seq, GROUP_W), BF16),
        scratch_shapes=[pltpu.VMEM((tm, LANES), F32)] * 4,
        compiler_params=_params(("parallel", "parallel")),
        name="attn_merge",
    )(*args)
    return attn.reshape(batch * seq, GROUP_W)


def _merge_body(x_ref, attn_ref, hcn_ref, wga_ref, wgb_ref, wao_ref, wco_ref, wout_ref, g_ref, b_ref,
                x1_ref, x1t_ref, xb_ref, acc_ref):
    j = pl.program_id(1)

    @pl.when(j == 0)
    def _():
        xb_ref[...] = x_ref[...].astype(BF16)
        acc_ref[...] = jnp.zeros_like(acc_ref)

    xb = xb_ref[...]
    ga = jax.nn.sigmoid(jnp.dot(xb, wga_ref[...], preferred_element_type=F32))
    gb = jax.nn.sigmoid(jnp.dot(xb, wgb_ref[...], preferred_element_type=F32))
    br_a = jnp.dot(attn_ref[...], wao_ref[...], preferred_element_type=F32)
    br_b = jnp.dot(hcn_ref[...], wco_ref[...], preferred_element_type=F32)
    merged = (ga * br_a + gb * br_b).astype(BF16)
    acc_ref[...] += jnp.dot(merged, wout_ref[...], preferred_element_type=F32)

    @pl.when(j == pl.num_programs(1) - 1)
    def _():
        x1 = _layer_norm(ALPHA * x_ref[...] + acc_ref[...], g_ref[...], b_ref[...])
        x1_ref[...] = x1
        x1t_ref[...] = x1.T.astype(BF16)


def _merge_layer(x2, attn, hcn, w_in_bf, w_ao_bf, w_co_bf, w_out_bf, ln_g, ln_b):
    t = x2.shape[0]
    tm, tn = 512, 512
    nj = D // tn
    ga0 = GATE_COL0 // tn
    gb0 = (GATE_COL0 + D) // tn
    vec = pl.BlockSpec((1, D), lambda i, j: (0, 0))
    return pl.pallas_call(
        _merge_body,
        grid=(t // tm, nj),
        in_specs=[
            pl.BlockSpec((tm, D), lambda i, j: (i, 0)),
            pl.BlockSpec((tm, GROUP_W), lambda i, j: (i, 0)),
            pl.BlockSpec((tm, CONV_C), lambda i, j: (i, 0)),
            pl.BlockSpec((D, tn), lambda i, j: (0, ga0 + j)),
            pl.BlockSpec((D, tn), lambda i, j: (0, gb0 + j)),
            pl.BlockSpec((GROUP_W, tn), lambda i, j: (0, j)),
            pl.BlockSpec((CONV_C, tn), lambda i, j: (0, j)),
            pl.BlockSpec((tn, D), lambda i, j: (j, 0)),
            vec, vec,
        ],
        out_specs=[pl.BlockSpec((tm, D), lambda i, j: (i, 0)),
                   pl.BlockSpec((D, tm), lambda i, j: (0, i))],
        out_shape=[jax.ShapeDtypeStruct((t, D), F32), jax.ShapeDtypeStruct((D, t), BF16)],
        scratch_shapes=[pltpu.VMEM((tm, D), BF16), pltpu.VMEM((tm, D), F32)],
        compiler_params=_params(("parallel", "arbitrary")),
        name="merge_layer",
    )(x2, attn, hcn, w_in_bf, w_in_bf, w_ao_bf, w_co_bf, w_out_bf, ln_g.reshape(1, -1), ln_b.reshape(1, -1))


RANK_CODE_BASE = -(2.0 ** 120)


def _top16(s, exact_ties):
    row = lax.broadcasted_iota(jnp.int32, s.shape, 0).astype(F32)
    vals = []
    for k in range(TOPK):
        m = jnp.max(s, axis=0, keepdims=True)
        hit = s == m
        if exact_ties:
            hit = row == jnp.min(jnp.where(hit, row, float(N_KEYS)), axis=0, keepdims=True)
        s = jnp.where(hit, RANK_CODE_BASE * (1.0 + k / TOPK), s)
        vals.append(m)
    taken = s <= RANK_CODE_BASE
    rank = jnp.where(taken, (s * (1.0 / RANK_CODE_BASE) - 1.0) * TOPK, float(TOPK))
    n_taken = jnp.sum(jnp.where(taken, 1.0, 0.0), axis=0, keepdims=True)
    return rank, jnp.concatenate(vals, axis=0), n_taken


def _pair_counts(a1, a2, exact_ties):
    assert TOPK == 16 and SUBLANES == 8
    row = lax.broadcasted_iota(jnp.int32, (SUBLANES, LANES), 0)
    lo, hi = a1[0:SUBLANES, :], a1[SUBLANES:, :]
    first = a1[0:1, :]
    tail = jnp.roll(a2[SUBLANES:, :], 2, 0)
    blocks = [
        lo + a2[0:1, :],
        hi + a2[0:1, :],
        lo + a2[1:2, :],
        jnp.where(row < 5, lo + a2[2:3, :], jnp.roll(lo, 5, 0) + a2[4:5, :]),
        jnp.where(row < 4, lo + a2[3:4, :],
                  jnp.where(row < 6, jnp.roll(lo, 4, 0) + a2[5:6, :], jnp.roll(lo, 6, 0) + a2[6:7, :])),
        jnp.where(row < 2, lo + a2[7:8, :], first + tail),
        jnp.where(row < 2, first + tail, NEG_INF),
    ]
    ra = [row, row + SUBLANES, row, jnp.where(row < 5, row, row - 5),
          jnp.where(row < 4, row, jnp.where(row < 6, row - 4, row - 6)),
          jnp.where(row < 2, row, 0), jnp.zeros_like(row)]
    rb = [jnp.zeros_like(row), jnp.zeros_like(row), jnp.ones_like(row), jnp.where(row < 5, 2, 4),
          jnp.where(row < 4, 3, jnp.where(row < 6, 5, 6)),
          jnp.where(row < 2, 7, row + 6), row + 14]
    cand0 = jnp.concatenate(blocks, axis=0)
    pos = jnp.concatenate([a * TOPK + b for a, b in zip(ra, rb)], axis=0).astype(F32)
    top = a1[0:1, :] + a2[0:1, :]
    cand = cand0
    for _ in range(TOPK):
        m = jnp.max(cand, axis=0, keepdims=True)
        hit = cand == m
        if exact_ties:
            hit = pos == jnp.min(jnp.where(hit, pos, float(TOPK * TOPK)), axis=0, keepdims=True)
        cand = jnp.where(hit, NEG_INF, cand)
    chosen = cand != cand0
    z = jnp.sum(jnp.where(chosen, jnp.exp(cand0 - top), 0.0), axis=0, keepdims=True)
    cnt = [jnp.where(chosen[k * SUBLANES:(k + 1) * SUBLANES, :], 1.0, 0.0) for k in range(len(blocks))]
    extra = (jnp.sum(jnp.where(row >= 2, cnt[5], 0.0), axis=0, keepdims=True)
             + jnp.sum(cnt[6], axis=0, keepdims=True))
    n_lo = (cnt[0] + cnt[2]
            + jnp.where(row < 5, cnt[3], 0.0) + jnp.roll(jnp.where(row >= 5, cnt[3], 0.0), 3, 0)
            + jnp.where(row < 4, cnt[4], 0.0)
            + jnp.roll(jnp.where((row >= 4) & (row < 6), cnt[4], 0.0), 4, 0)
            + jnp.roll(jnp.where(row >= 6, cnt[4], 0.0), 2, 0)
            + jnp.where(row < 2, cnt[5], 0.0)
            + jnp.where(row == 0, extra, 0.0))
    n = jnp.concatenate([n_lo, cnt[1]], axis=0)
    return n, z, jnp.sum(n, axis=0, keepdims=True)


def _route_body(x1t_ref, wq_ref, keys_ref, r2_ref, e2_ref, n1_ref, e1_ref, q_ref, sc_ref):
    tm = x1t_ref.shape[1]
    ns = tm // LANES
    q_ref[...] = jnp.dot(wq_ref[...], x1t_ref[...], preferred_element_type=F32).astype(BF16)
    for hp in range(2 * PEER_HEADS):
        st = jnp.dot(keys_ref[hp % 2], q_ref[hp * N_KEYS:(hp + 1) * N_KEYS, :],
                     preferred_element_type=F32)
        for sl in range(ns):
            sc_ref[hp // 2, sl, hp % 2] = st[:, sl * LANES:(sl + 1) * LANES]

    def per_head_slab(idx, carry):
        h = idx // ns
        sl = idx % ns

        def route(exact_ties):
            s1 = sc_ref[h, sl, 0]
            s2 = sc_ref[h, sl, 1]
            r1, a1, t1 = _top16(s1, exact_ties)
            r2, a2, t2 = _top16(s2, exact_ties)
            n, z, tc = _pair_counts(a1, a2, exact_ties)
            n1 = jnp.zeros_like(r1)
            for a in range(TOPK):
                n1 = jnp.where(r1 == float(a), n[a:a + 1, :], n1)
            e1 = jnp.where(r1 < float(TOPK), jnp.exp(s1 - a1[0:1, :]), 0.0)
            e2 = jnp.where(r2 < float(TOPK), jnp.exp(s2 - a2[0:1, :]), 0.0) / z
            r2_ref[sl, h] = r2.astype(r2_ref.dtype)
            e2_ref[sl, h] = e2.astype(e2_ref.dtype)
            n1_ref[sl, h] = n1
            e1_ref[sl, h] = e1
            return (t1 == float(TOPK)) & (t2 == float(TOPK)) & (tc == float(TOPK))

        ok = route(False)

        @pl.when(jnp.max(jnp.where(ok, 0.0, 1.0)) > 0.0)
        def _():
            route(True)

        return carry

    lax.fori_loop(0, PEER_HEADS * ns, per_head_slab, 0)


def _peer_route(x1t, wq_t_bf, keys_bf):
    t = x1t.shape[1]
    tm = 512
    ns = tm // LANES
    blk = pl.BlockSpec((ns, PEER_HEADS, N_KEYS, LANES), lambda i: (i, 0, 0, 0))
    shape = (t // LANES, PEER_HEADS, N_KEYS, LANES)
    return pl.pallas_call(
        _route_body,
        grid=(t // tm,),
        in_specs=[
            pl.BlockSpec((D, tm), lambda i: (0, i)),
            pl.BlockSpec((D, D), lambda i: (0, 0)),
            pl.BlockSpec((2, N_KEYS, N_KEYS), lambda i: (0, 0, 0)),
        ],
        out_specs=[blk, blk, blk, blk],
        out_shape=[jax.ShapeDtypeStruct(shape, BF16), jax.ShapeDtypeStruct(shape, BF16),
                   jax.ShapeDtypeStruct(shape, F32), jax.ShapeDtypeStruct(shape, F32)],
        scratch_shapes=[pltpu.VMEM((D, tm), BF16), pltpu.VMEM((PEER_HEADS, ns, 2, N_KEYS, LANES), F32)],
        compiler_params=_params(("parallel",)),
        name="peer_route",
    )(x1t, wq_t_bf, keys_bf)


def _experts_body(x1t_ref, x1_ref, u_ref, vt_ref, r2_ref, e2_ref, n1_ref, e1_ref, g_ref, b_ref,
                  o_ref, acc_ref, gh_ref):
    c = pl.program_id(1)
    tm = x1t_ref.shape[1]
    ns = tm // LANES
    ni = u_ref.shape[0] // N_KEYS

    @pl.when(c == 0)
    def _():
        acc_ref[...] = jnp.zeros_like(acc_ref)

    nb = N_KEYS // BF16_ROWS
    zero = jnp.zeros((BF16_ROWS, LANES), BF16)
    for i0 in range(0, ni, GATE_ROWS_LIVE):
        for ts in range(ns):
            cs = slice(ts * LANES, (ts + 1) * LANES)
            gate = [[None] * nb for _ in range(GATE_ROWS_LIVE)]
            for h in range(PEER_HEADS):
                r2 = [r2_ref[ts, h, jb * BF16_ROWS:(jb + 1) * BF16_ROWS, :] for jb in range(nb)]
                e2 = [e2_ref[ts, h, jb * BF16_ROWS:(jb + 1) * BF16_ROWS, :] for jb in range(nb)]
                for ii in range(GATE_ROWS_LIVE):
                    i = (c % (SUBLANES // ni)) * ni + i0 + ii
                    n1 = jnp.broadcast_to(n1_ref[ts, h, pl.ds(i, 1), :], (BF16_ROWS, LANES)).astype(BF16)
                    e1 = jnp.broadcast_to(e1_ref[ts, h, pl.ds(i, 1), :], (BF16_ROWS, LANES)).astype(BF16)
                    for jb in range(nb):
                        term = jnp.minimum(jnp.maximum(n1 - r2[jb], zero), e1) * e2[jb]
                        gate[ii][jb] = term if h == 0 else gate[ii][jb] + term
            for ii in range(GATE_ROWS_LIVE):
                for jb in range(nb):
                    r0 = (i0 + ii) * N_KEYS + jb * BF16_ROWS
                    gh_ref[r0:r0 + BF16_ROWS, cs] = gate[ii][jb]
    for t0 in range(0, tm, TOKEN_COLS):
        cs = slice(t0, t0 + TOKEN_COLS)
        ht = jnp.dot(u_ref[...], x1t_ref[:, cs], preferred_element_type=F32)
        hb = ht.astype(BF16)
        act = (0.5 * hb) * (1.0 + lax.erf(hb * (1.0 / math.sqrt(2.0))))
        gh_ref[:, cs] = gh_ref[:, cs] * act
    for t0 in range(0, tm, TOKEN_COLS):
        cs = slice(t0, t0 + TOKEN_COLS)
        for r0 in range(0, D, ACC_ROWS):
            acc_ref[r0:r0 + ACC_ROWS, cs] += jnp.dot(vt_ref[r0:r0 + ACC_ROWS, :], gh_ref[:, cs],
                                                     preferred_element_type=F32)

    @pl.when(c == pl.num_programs(1) - 1)
    def _():
        y = ALPHA * x1_ref[...] + acc_ref[...].T
        o_ref[...] = _layer_norm(y, g_ref[...], b_ref[...])


def _peer_experts(x1t, x1, u_bf, vt_bf, route, ln_g, ln_b):
    t = x1.shape[0]
    tm, ec = 1024, EXPERT_CHUNK
    ns = tm // LANES
    r2, e2, n1, e1 = route
    ni = ec // N_KEYS
    assert SUBLANES % ni == 0
    once = pl.Buffered(1)
    blk = pl.BlockSpec((ns, PEER_HEADS, N_KEYS, LANES), lambda i, c: (i, 0, 0, 0), pipeline_mode=once)
    row = pl.BlockSpec((ns, PEER_HEADS, SUBLANES, LANES), lambda i, c: (i, 0, c // (SUBLANES // ni), 0))
    vec = pl.BlockSpec((1, D), lambda i, c: (0, 0))
    return pl.pallas_call(
        _experts_body,
        grid=(t // tm, N_EXPERTS // ec),
        in_specs=[
            pl.BlockSpec((D, tm), lambda i, c: (0, i)),
            pl.BlockSpec((tm, D), lambda i, c: (i, 0), pipeline_mode=once),
            pl.BlockSpec((ec, D), lambda i, c: (c, 0)),
            pl.BlockSpec((None, D, ec), lambda i, c: (c, 0, 0)),
            blk, blk, row, row, vec, vec,
        ],
        out_specs=pl.BlockSpec((tm, D), lambda i, c: (i, 0), pipeline_mode=once),
        out_shape=jax.ShapeDtypeStruct((t, D), F32),
        scratch_shapes=[pltpu.VMEM((D, tm), F32), pltpu.VMEM((ec, tm), BF16)],
        compiler_params=_params(("parallel", "arbitrary")),
        name="peer_experts",
    )(x1t, x1, u_bf, vt_bf, r2, e2, n1, e1, ln_g.reshape(1, -1), ln_b.reshape(1, -1))


def _rope_tables(seq):
    half = HEAD_DIM // 2
    inv_freq = ROPE_THETA ** (-jnp.arange(half, dtype=F32) / half)
    ang = jnp.arange(seq, dtype=F32)[:, None] * inv_freq[None, :]
    cos, sin = jnp.cos(ang), jnp.sin(ang)
    reps = LANES // HEAD_DIM
    cos_t = jnp.tile(jnp.concatenate([cos, cos], axis=1), (1, reps))
    sin_t = jnp.tile(jnp.concatenate([-sin, sin], axis=1), (1, reps))
    scale = 1.0 / math.sqrt(HEAD_DIM)
    return (jnp.stack([cos_t * scale, cos_t, jnp.ones_like(cos_t)]),
            jnp.stack([sin_t * scale, sin_t, jnp.zeros_like(sin_t)]))


def _encoder_layer(x, w, tables):
    batch, seq, _ = x.shape
    x2 = x.reshape(batch * seq, D)
    qkv = _qkv_proj(x2, w["w_in"], tables[0], tables[1], batch, seq)
    hc = _glu_proj(x2, w["w_in"])
    attn_parts = [_dilated_attention(qkv[g], g) for g in range(N_GROUPS)]
    hcn = _conv_module(hc, w["conv_w"], w["conv_b"], w["conv_ln_g"], w["conv_ln_b"], batch, seq)
    attn = _attn_merge(attn_parts, batch, seq)
    x1, x1t = _merge_layer(x2, attn, hcn, w["w_in"], w["w_attn_o"], w["w_conv_o"], w["w_out"],
                           w["ln1_g"], w["ln1_b"])
    route = _peer_route(x1t, w["wq_t"], w["keys"])
    y = _peer_experts(x1t, x1, w["u"], w["v_t"], route, w["ln2_g"], w["ln2_b"])
    return y.reshape(batch, seq, D)


def kernel(x_prompt, x_sample, w_in, w_attn_o, conv_w, conv_b, conv_ln_g, conv_ln_b, w_conv_o, w_out,
           ln1_g, ln1_b, peer_w_q, peer_sub_keys, peer_u, peer_v, ln2_g, ln2_b):
    assert w_in.shape[0] == DEPTH == 1
    w = {
        "w_in": w_in[0].astype(BF16),
        "w_attn_o": w_attn_o[0].astype(BF16),
        "w_conv_o": w_conv_o[0].astype(BF16),
        "w_out": w_out[0].astype(BF16),
        "conv_w": conv_w[0], "conv_b": conv_b[0], "conv_ln_g": conv_ln_g[0], "conv_ln_b": conv_ln_b[0],
        "ln1_g": ln1_g[0], "ln1_b": ln1_b[0], "ln2_g": ln2_g[0], "ln2_b": ln2_b[0],
        "wq_t": peer_w_q[0].T.astype(BF16),
        "keys": peer_sub_keys[0].astype(BF16),
        "u": peer_u[0].astype(BF16),
        "v_t": peer_v[0].reshape(N_EXPERTS // EXPERT_CHUNK, EXPERT_CHUNK, D).transpose(0, 2, 1).astype(BF16),
    }
    tables = _rope_tables(x_prompt.shape[1])
    assert x_sample.shape[1] == x_prompt.shape[1]
    return (_encoder_layer(x_prompt, w, tables), _encoder_layer(x_sample, w, tables))
```
